```python
import math
import jax
import jax.numpy as jnp
from jax import lax
import numpy as np

D_MODEL = 2048
BATCH = 8
SEQ = 2048
DEPTH = 4

CTX_LEN = 256
GRID_W = 64
HEAD_DIM = 128
N_HEAD_SLOTS = D_MODEL // HEAD_DIM
GDN_HEADS = N_HEAD_SLOTS // 2
GDN_DK = HEAD_DIM
GDN_DV = HEAD_DIM
GDN_CONV = 3
GDN_CHUNK = 64
DIFF_HEADS = N_HEAD_SLOTS // 2
DIFF_DH = HEAD_DIM // 2
Q_BLOCK = 128
NA_HEADS = N_HEAD_SLOTS // 2
NA_ROWS = 8
NA_COLS = 16
NA_QCOLS = 16
NA_KSPAN = 32
WIN_HEADS = N_HEAD_SLOTS // 2
WIN_KV_HEADS = 2
WIN = 128
WIN_BLOCK = 128
FFN_HIDDEN = -(-8 * D_MODEL // (3 * 256)) * 256
ROPE_BASE = 10000.0
NORM_EPS = 1e-6

GDN_QKV = GDN_HEADS * (2 * GDN_DK + GDN_DV)
EVEN_WIDTHS = (GDN_QKV, GDN_HEADS * GDN_DV, 2 * GDN_HEADS, 2 * GDN_HEADS,
               DIFF_HEADS * 2 * DIFF_DH, DIFF_HEADS * 2 * DIFF_DH, DIFF_HEADS * 2 * DIFF_DH)
ODD_WIDTHS = (NA_HEADS * HEAD_DIM, NA_HEADS * HEAD_DIM, NA_HEADS * HEAD_DIM,
              WIN_HEADS * HEAD_DIM, WIN_KV_HEADS * HEAD_DIM, WIN_KV_HEADS * HEAD_DIM)

kernel_name = 'hybrid_gdn_diff_natten_swa_dit_trunk'


def rmsnorm(x, g):
    xf = x.astype(jnp.float32)
    y = xf * lax.rsqrt(jnp.mean(xf * xf, axis=-1, keepdims=True) + NORM_EPS)
    return (y * g.astype(jnp.float32)).astype(x.dtype)


def l2norm(x):
    xf = x.astype(jnp.float32)
    return (xf * lax.rsqrt(jnp.sum(xf * xf, axis=-1, keepdims=True) + NORM_EPS)).astype(x.dtype)


def split_cols(p, widths):
    return jnp.split(p, np.cumsum(widths)[:-1].tolist(), axis=-1)


def _rope_axis(x, pos):
    n = x.shape[-1]
    inv = ROPE_BASE ** (-jnp.arange(0, n, 2, dtype=jnp.float32) / n)
    ang = pos.astype(jnp.float32)[:, None] * inv[None, :]
    bshape = (pos.shape[0],) + (1,) * (x.ndim - 3) + (n // 2,)
    cos = jnp.cos(ang).reshape(bshape).astype(x.dtype)
    sin = jnp.sin(ang).reshape(bshape).astype(x.dtype)
    x1, x2 = x[..., : n // 2], x[..., n // 2:]
    return jnp.concatenate([x1 * cos - x2 * sin, x2 * cos + x1 * sin], axis=-1)


def rope2d(x, rows, cols):
    d = x.shape[-1]
    return jnp.concatenate([_rope_axis(x[..., : d // 2], rows), _rope_axis(x[..., d // 2:], cols)], axis=-1)


def short_conv(x, w):
    k = w.shape[0]
    return lax.conv_general_dilated(x, w[:, None, :].astype(x.dtype), window_strides=(1,),
                                    padding=[(k // 2, k // 2)], dimension_numbers=('NWC', 'WIO', 'NWC'),
                                    feature_group_count=x.shape[-1])


def gated_delta_chunked(q, k, v, g, beta, s0):
    bsz, n, h, dk = q.shape
    dv = v.shape[-1]
    c = GDN_CHUNK
    nc = n // c

    def chunks(a):
        a = a.astype(jnp.float32).reshape((bsz, nc, c, h) + a.shape[3:])
        return jnp.moveaxis(a, 3, 1)

    qc = chunks(q) * dk ** -0.5
    kc = chunks(k)
    vc = chunks(v)
    bc = chunks(beta)
    gc = jnp.cumsum(chunks(g), axis=-1)
    tril = jnp.asarray(np.tril(np.ones((c, c), bool)))
    strict = jnp.asarray(np.tril(np.ones((c, c), bool), -1))
    decay = jnp.exp(jnp.where(tril, gc[..., :, None] - gc[..., None, :], -jnp.inf))
    kb = kc * bc[..., None]
    a_mat = jnp.where(strict, jnp.einsum('bhnid,bhnjd->bhnij', kb, kc) * decay, 0.0)
    eye = jnp.eye(c, dtype=jnp.float32)
    t_mat = lax.linalg.triangular_solve(eye + a_mat, jnp.broadcast_to(eye, a_mat.shape),
                                        left_side=True, lower=True)
    u = jnp.einsum('bhnij,bhnjd->bhnid', t_mat, vc * bc[..., None])
    w = jnp.einsum('bhnij,bhnjd->bhnid', t_mat, kb * jnp.exp(gc)[..., None])
    attn = jnp.where(tril, jnp.einsum('bhnid,bhnjd->bhnij', qc, kc) * decay, 0.0)
    qg = qc * jnp.exp(gc)[..., None]
    kdec = kc * jnp.exp(gc[..., -1:] - gc)[..., None]
    g_last = jnp.exp(gc[..., -1])

    def step(s, xs):
        qg_i, kdec_i, u_i, w_i, attn_i, gl_i = xs
        v_new = u_i - jnp.einsum('bhcd,bhde->bhce', w_i, s)
        o_i = jnp.einsum('bhcd,bhde->bhce', qg_i, s) + jnp.einsum('bhij,bhje->bhie', attn_i, v_new)
        s = s * gl_i[..., None, None] + jnp.einsum('bhcd,bhce->bhde', kdec_i, v_new)
        return s, o_i

    xs = tuple(jnp.moveaxis(a, 2, 0) for a in (qg, kdec, u, w, attn, g_last))
    s_fin, o = lax.scan(step, s0.astype(jnp.float32), xs)
    o = jnp.moveaxis(o, 0, 2).reshape(bsz, h, n, dv).transpose(0, 2, 1, 3)
    return o.astype(v.dtype), s_fin


def gdn_bidirectional(q, k, v, g, beta, qc, kc, vc, gc, betac):
    bsz, _, h, dk = q.shape
    s0 = jnp.zeros((bsz, h, dk, v.shape[-1]), jnp.float32)
    flip = lambda a: jnp.flip(a, axis=1)
    oc_f, sc_f = gated_delta_chunked(qc, kc, vc, gc[:, :, 0], betac[:, :, 0], s0)
    o_f, _ = gated_delta_chunked(q, k, v, g[:, :, 0], beta[:, :, 0], sc_f)
    oc_b, sc_b = gated_delta_chunked(flip(qc), flip(kc), flip(vc), flip(gc[:, :, 1]), flip(betac[:, :, 1]), s0)
    o_b, _ = gated_delta_chunked(flip(q), flip(k), flip(v), flip(g[:, :, 1]), flip(beta[:, :, 1]), sc_b)
    return o_f + flip(o_b), oc_f + flip(oc_b)


def diff_attention(q, k, v, qc, kc, vc, lam, rows, cols, need_ctx):
    bsz, n, h = q.shape[:3]
    q = rope2d(q, rows, cols)
    k = rope2d(k, rows, cols)
    k_all = jnp.concatenate([k, kc], axis=1)
    v_all = jnp.concatenate([v, vc], axis=1)
    scale = q.shape[-1] ** -0.5

    def attend(qb, kk, vv):
        s = jnp.einsum('bqhcd,bkhcd->bhcqk', qb, kk).astype(jnp.float32) * scale
        p = jax.nn.softmax(s, axis=-1)
        a = (p[:, :, 0] - lam * p[:, :, 1]).astype(vv.dtype)
        return jnp.einsum('bhqk,bkhe->bqhe', a, vv)

    qb = jnp.moveaxis(q.reshape((bsz, n // Q_BLOCK, Q_BLOCK) + q.shape[2:]), 1, 0)
    o = lax.map(lambda blk: attend(blk, k_all, v_all), qb)
    o = jnp.moveaxis(o, 0, 1).reshape(bsz, n, h, v.shape[-1])
    oc = attend(qc, kc, vc) if need_ctx else None
    return o, oc


def neighbourhood_attention(q, k, v, kc, vc, rpb):
    bsz, n, h, d = q.shape
    n_rows = n // GRID_W
    kr = min(NA_ROWS, n_rows)
    scale = d ** -0.5
    ncb = GRID_W // NA_QCOLS
    q_cols = np.arange(GRID_W).reshape(ncb, NA_QCOLS)
    win_c = np.clip(q_cols - NA_COLS // 2, 0, GRID_W - NA_COLS)
    span0 = np.clip(np.arange(ncb) * NA_QCOLS - NA_COLS // 2, 0, GRID_W - NA_KSPAN)
    key_cols = span0[:, None] + np.arange(NA_KSPAN)
    kcol = key_cols[:, None, :]
    col_ok = (kcol >= win_c[..., None]) & (kcol < win_c[..., None] + NA_COLS)
    dc_idx = np.clip(kcol - q_cols[..., None] + NA_COLS - 1, 0, 2 * NA_COLS - 2)
    nl = kr * NA_KSPAN
    mask = jnp.asarray(np.broadcast_to(col_ok[:, :, None, :], (ncb, NA_QCOLS, kr, NA_KSPAN)).reshape(ncb, NA_QCOLS, nl))
    rpb_c = rpb[:, :, dc_idx]
    kg = k.reshape(bsz, n_rows, GRID_W, h, d)
    vg = v.reshape(bsz, n_rows, GRID_W, h, d)

    def row_block(args):
        r, q_r = args
        start = jnp.clip(r - kr // 2, 0, n_rows - kr)

        def gather(a):
            a = lax.dynamic_slice_in_dim(a, start, kr, axis=1)[:, :, key_cols]
            return a.transpose(0, 2, 1, 3, 4, 5).reshape(bsz, ncb, nl, h, d)

        k_r, v_r = gather(kg), gather(vg)
        q_b = q_r.reshape(bsz, ncb, NA_QCOLS, h, d)
        dr_idx = start + jnp.arange(kr) - r + NA_ROWS - 1
        bias = rpb_c[:, dr_idx].transpose(0, 2, 3, 1, 4).reshape(h, ncb, NA_QCOLS, nl)
        s_loc = jnp.einsum('bnqhd,bnkhd->bhnqk', q_b, k_r).astype(jnp.float32) * scale + bias
        s_loc = jnp.where(mask, s_loc, -jnp.inf)
        s_ctx = jnp.einsum('bnqhd,bkhd->bhnqk', q_b, kc).astype(jnp.float32) * scale
        p = jax.nn.softmax(jnp.concatenate([s_loc, s_ctx], axis=-1), axis=-1).astype(v.dtype)
        o = (jnp.einsum('bhnqk,bnkhd->bnqhd', p[..., :nl], v_r)
             + jnp.einsum('bhnqk,bkhd->bnqhd', p[..., nl:], vc))
        return o.reshape(bsz, GRID_W, h, d)

    qg = jnp.moveaxis(q.reshape(bsz, n_rows, GRID_W, h, d), 1, 0)
    o = lax.map(row_block, (jnp.arange(n_rows), qg))
    return jnp.moveaxis(o, 0, 1).reshape(bsz, n, h, d)


def context_attention(qc, kc, vc):
    s = jnp.einsum('bqhd,bkhd->bhqk', qc, kc).astype(jnp.float32) * qc.shape[-1] ** -0.5
    p = jax.nn.softmax(s, axis=-1).astype(vc.dtype)
    return jnp.einsum('bhqk,bkhd->bqhd', p, vc)


def window_sink_attention(q, k, v, kc, vc, sink, rows, cols):
    bsz, n, h, d = q.shape
    kvh = k.shape[2]
    grp = h // kvh
    q = rope2d(q, rows, cols)
    k = rope2d(k, rows, cols)
    scale = d ** -0.5
    nb = n // WIN_BLOCK

    def band(a):
        ap = jnp.pad(a, ((0, 0), (WIN_BLOCK, WIN_BLOCK), (0, 0), (0, 0))).reshape(bsz, nb + 2, WIN_BLOCK, kvh, d)
        return jnp.concatenate([ap[:, :-2], ap[:, 1:-1], ap[:, 2:]], axis=2)

    kb, vb = band(k), band(v)
    qb = q.reshape(bsz, nb, WIN_BLOCK, kvh, grp, d)
    qpos = jnp.arange(n).reshape(nb, WIN_BLOCK)
    kpos = (jnp.arange(nb) * WIN_BLOCK - WIN_BLOCK)[:, None] + jnp.arange(3 * WIN_BLOCK)[None, :]
    kp = kpos[:, None, :]
    valid = (kp >= 0) & (kp < n) & (jnp.abs(qpos[:, :, None] - kp) <= WIN)
    s_loc = jnp.where(valid, jnp.einsum('bnqhgd,bnkhd->bhgnqk', qb, kb).astype(jnp.float32) * scale, -jnp.inf)
    s_ctx = jnp.einsum('bnqhgd,bkhd->bhgnqk', qb, kc).astype(jnp.float32) * scale
    sink_l = jnp.broadcast_to(sink.astype(jnp.float32).reshape(1, kvh, grp, 1, 1, 1), s_ctx.shape[:-1] + (1,))
    p = jax.nn.softmax(jnp.concatenate([s_loc, s_ctx, sink_l], axis=-1), axis=-1).astype(v.dtype)
    nl = 3 * WIN_BLOCK
    o = (jnp.einsum('bhgnqk,bnkhd->bnqhgd', p[..., :nl], vb)
         + jnp.einsum('bhgnqk,bkhd->bnqhgd', p[..., nl:-1], vc))
    return o.reshape(bsz, n, h, d)


def context_sink_attention(qc, kc, vc, sink):
    bsz, m, h, d = qc.shape
    kvh = kc.shape[2]
    grp = h // kvh
    qg = qc.reshape(bsz, m, kvh, grp, d)
    s = jnp.einsum('bqhgd,bkhd->bhgqk', qg, kc).astype(jnp.float32) * d ** -0.5
    sink_l = jnp.broadcast_to(sink.astype(jnp.float32).reshape(1, kvh, grp, 1, 1), s.shape[:-1] + (1,))
    p = jax.nn.softmax(jnp.concatenate([s, sink_l], axis=-1), axis=-1).astype(vc.dtype)[..., :-1]
    return jnp.einsum('bhgqk,bkhd->bqhgd', p, vc).reshape(bsz, m, h, d)


def swiglu(u, wg, wu, wd):
    return (jax.nn.silu(u @ wg) * (u @ wu)) @ wd


def even_mixer(u, uc, w_in, conv_w, a_log, dt_bias, gdn_g, lam_q1, lam_k1, lam_q2, lam_k2, subln_g,
               lambda_init, rows, cols, need_ctx):
    def project(t):
        bsz, n = t.shape[:2]
        qkv, z, a, b, dq, dk, dv = split_cols(t @ w_in, EVEN_WIDTHS)
        qkv = jax.nn.silu(short_conv(qkv, conv_w))
        gq, gk, gv = split_cols(qkv, (GDN_HEADS * GDN_DK, GDN_HEADS * GDN_DK, GDN_HEADS * GDN_DV))
        gq = l2norm(gq.reshape(bsz, n, GDN_HEADS, GDN_DK))
        gk = l2norm(gk.reshape(bsz, n, GDN_HEADS, GDN_DK))
        gv = gv.reshape(bsz, n, GDN_HEADS, GDN_DV)
        g = -jnp.exp(a_log.astype(jnp.float32)) * jax.nn.softplus(
            a.reshape(bsz, n, 2, GDN_HEADS).astype(jnp.float32) + dt_bias.astype(jnp.float32))
        beta = jax.nn.sigmoid(b.reshape(bsz, n, 2, GDN_HEADS).astype(jnp.float32))
        gdn = (gq, gk, gv, g, beta, z.reshape(bsz, n, GDN_HEADS, GDN_DV))
        diff = (dq.reshape(bsz, n, DIFF_HEADS, 2, DIFF_DH), dk.reshape(bsz, n, DIFF_HEADS, 2, DIFF_DH),
                dv.reshape(bsz, n, DIFF_HEADS, 2 * DIFF_DH))
        return gdn, diff

    (q, k, v, g, beta, z), (dq, dk, dv) = project(u)
    (qc, kc, vc, gc, betac, zc), (dqc, dkc, dvc) = project(uc)
    o, oc = gdn_bidirectional(q, k, v, g, beta, qc, kc, vc, gc, betac)
    lam = (jnp.exp(jnp.sum(lam_q1.astype(jnp.float32) * lam_k1.astype(jnp.float32)))
           - jnp.exp(jnp.sum(lam_q2.astype(jnp.float32) * lam_k2.astype(jnp.float32))) + lambda_init)
    d_o, d_oc = diff_attention(dq, dk, dv, dqc, dkc, dvc, lam, rows, cols, need_ctx)

    def merge(o_gdn, zz, o_diff):
        bsz, n = o_gdn.shape[:2]
        a_out = rmsnorm(o_gdn, gdn_g) * jax.nn.silu(zz)
        b_out = rmsnorm(o_diff, subln_g) * (1.0 - lambda_init)
        return jnp.concatenate([a_out.reshape(bsz, n, -1), b_out.reshape(bsz, n, -1)], axis=-1)

    y = merge(o, z, d_o)
    yc = merge(oc, zc, d_oc) if need_ctx else None
    return y, yc


def odd_mixer(u, uc, w_in, rpb, sink, rows, cols, need_ctx):
    def project(t):
        bsz, n = t.shape[:2]
        nq, nk, nv, wq, wk, wv = split_cols(t @ w_in, ODD_WIDTHS)
        hd = lambda a, nh: a.reshape(bsz, n, nh, HEAD_DIM)
        return ((hd(nq, NA_HEADS), hd(nk, NA_HEADS), hd(nv, NA_HEADS)),
                (hd(wq, WIN_HEADS), hd(wk, WIN_KV_HEADS), hd(wv, WIN_KV_HEADS)))

    (nq, nk, nv), (wq, wk, wv) = project(u)
    (nqc, nkc, nvc), (wqc, wkc, wvc) = project(uc)
    bsz, n = u.shape[:2]
    c_out = neighbourhood_attention(nq, nk, nv, nkc, nvc, rpb)
    d_out = window_sink_attention(wq, wk, wv, wkc, wvc, sink, rows, cols)
    y = jnp.concatenate([c_out.reshape(bsz, n, -1), d_out.reshape(bsz, n, -1)], axis=-1)
    yc = None
    if need_ctx:
        m = uc.shape[1]
        cc = context_attention(nqc, nkc, nvc)
        dc = context_sink_attention(wqc, wkc, wvc, sink)
        yc = jnp.concatenate([cc.reshape(bsz, m, -1), dc.reshape(bsz, m, -1)], axis=-1)
    return y, yc


def setup_inputs(seed: int = 0) -> dict:
    key = jax.random.key(seed)
    ks = list(jax.random.split(key, 28))
    n_even = (DEPTH + 1) // 2
    n_odd = DEPTH // 2
    f32 = jnp.float32
    d = D_MODEL

    def nrm(k, shape, std):
        return jax.random.normal(k, shape, f32) * std

    dt = jnp.exp(jax.random.uniform(ks[9], (n_even, 2, GDN_HEADS), f32, math.log(1e-3), math.log(1e-1)))
    dt_bias = dt + jnp.log(-jnp.expm1(-dt))
    return {
        'x': nrm(ks[0], (BATCH, SEQ, d), 1.0),
        'c': nrm(ks[1], (BATCH, d), 1.0),
        'ctx': nrm(ks[2], (BATCH, CTX_LEN, d), 1.0),
        'c_ctx': nrm(ks[3], (d,), 1.0),
        'ada_w': nrm(ks[4], (DEPTH, d, 6 * d), 0.5 * d ** -0.5),
        'ada_b': nrm(ks[5], (DEPTH, 6 * d), 0.02),
        'norm_mix_g': 1.0 + nrm(ks[6], (DEPTH, d), 0.02),
        'norm_ffn_g': 1.0 + nrm(ks[7], (DEPTH, d), 0.02),
        'w_in_even': nrm(ks[8], (n_even, d, sum(EVEN_WIDTHS)), d ** -0.5),
        'gdn_conv_w': nrm(ks[10], (n_even, GDN_CONV, GDN_QKV), GDN_CONV ** -0.5),
        'gdn_a_log': jnp.log(jax.random.uniform(ks[11], (n_even, 2, GDN_HEADS), f32, 1.0, 16.0)),
        'gdn_dt_bias': dt_bias,
        'gdn_norm_g': 1.0 + nrm(ks[12], (n_even, GDN_DV), 0.02),
        'diff_lambda_q1': nrm(ks[13], (n_even, DIFF_DH), 0.1),
        'diff_lambda_k1': nrm(ks[14], (n_even, DIFF_DH), 0.1),
        'diff_lambda_q2': nrm(ks[15], (n_even, DIFF_DH), 0.1),
        'diff_lambda_k2': nrm(ks[16], (n_even, DIFF_DH), 0.1),
        'diff_subln_g': 1.0 + nrm(ks[17], (n_even, 2 * DIFF_DH), 0.02),
        'w_in_odd': nrm(ks[18], (n_odd, d, sum(ODD_WIDTHS)), d ** -0.5),
        'na_rpb': nrm(ks[19], (n_odd, NA_HEADS, 2 * NA_ROWS - 1, 2 * NA_COLS - 1), 0.1),
        'win_sink': nrm(ks[20], (n_odd, WIN_HEADS), 0.5),
        'w_out': nrm(ks[21], (DEPTH, d, d), d ** -0.5),
        'ffn_w_gate': nrm(ks[22], (DEPTH, d, FFN_HIDDEN), d ** -0.5),
        'ffn_w_up': nrm(ks[23], (DEPTH, d, FFN_HIDDEN), d ** -0.5),
        'ffn_w_down': nrm(ks[24], (DEPTH, FFN_HIDDEN, d), FFN_HIDDEN ** -0.5),
        'final_norm_g': 1.0 + nrm(ks[25], (d,), 0.02),
    }


def reference(x, c, ctx, c_ctx, ada_w, ada_b, norm_mix_g, norm_ffn_g,
              w_in_even, gdn_conv_w, gdn_a_log, gdn_dt_bias, gdn_norm_g,
              diff_lambda_q1, diff_lambda_k1, diff_lambda_q2, diff_lambda_k2, diff_subln_g,
              w_in_odd, na_rpb, win_sink, w_out, ffn_w_gate, ffn_w_up, ffn_w_down, final_norm_g):
    n = x.shape[1]
    pos = jnp.arange(n)
    rows, cols = pos // GRID_W, pos % GRID_W
    h, hc = x, ctx
    c_act = jax.nn.silu(c)
    cc_act = jax.nn.silu(c_ctx)
    for l in range(DEPTH):
        need_ctx = l < DEPTH - 1
        mod = jnp.split((c_act @ ada_w[l] + ada_b[l])[:, None, :], 6, axis=-1)
        modc = jnp.split(cc_act @ ada_w[l] + ada_b[l], 6, axis=-1)
        u = rmsnorm(h, norm_mix_g[l]) * (1.0 + mod[1]) + mod[0]
        uc = rmsnorm(hc, norm_mix_g[l]) * (1.0 + modc[1]) + modc[0]
        i = l // 2
        if l % 2 == 0:
            y, yc = even_mixer(u, uc, w_in_even[i], gdn_conv_w[i], gdn_a_log[i], gdn_dt_bias[i], gdn_norm_g[i],
                               diff_lambda_q1[i], diff_lambda_k1[i], diff_lambda_q2[i], diff_lambda_k2[i],
                               diff_subln_g[i], 0.8 - 0.6 * math.exp(-0.3 * l), rows, cols, need_ctx)
        else:
            y, yc = odd_mixer(u, uc, w_in_odd[i], na_rpb[i], win_sink[i], rows, cols, need_ctx)
        h = h + mod[2] * (y @ w_out[l])
        u2 = rmsnorm(h, norm_ffn_g[l]) * (1.0 + mod[4]) + mod[3]
        h = h + mod[5] * swiglu(u2, ffn_w_gate[l], ffn_w_up[l], ffn_w_down[l])
        if need_ctx:
            hc = hc + modc[2] * (yc @ w_out[l])
            u2c = rmsnorm(hc, norm_ffn_g[l]) * (1.0 + modc[4]) + modc[3]
            hc = hc + modc[5] * swiglu(u2c, ffn_w_gate[l], ffn_w_up[l], ffn_w_down[l])
    return rmsnorm(h, final_norm_g)
```

```python
import functools

import jax
import jax.numpy as jnp
import numpy as np
from jax import lax
from jax.experimental import pallas as pl
from jax.experimental.pallas import tpu as pltpu

F32 = jnp.float32
BF16 = jnp.bfloat16

DEPTH = 4
GRID_W = 64
HEAD_DIM = 128
N_HEADS_HALF = 8
HALF_W = N_HEADS_HALF * HEAD_DIM
GDN_CHUNK = 64
DIFF_DH = 64
NA_ROWS = 8
NA_COLS = 16
WIN = 128
WIN_BLOCK = 128
WIN_KV_HEADS = 2
WIN_GROUP = N_HEADS_HALF // WIN_KV_HEADS
ROPE_BASE = 10000.0
NORM_EPS = 1e-6
LOG2E = 1.4426950408889634
NEG_BIG = -1e30

V7X_VMEM_LIMIT = 56 * 1024 * 1024
MOD_ROWS = 16


def _cparams(sem):
    return pltpu.CompilerParams(dimension_semantics=sem, vmem_limit_bytes=V7X_VMEM_LIMIT)


def _bdot(a, b):
    return jnp.dot(a.astype(BF16), b.astype(BF16), preferred_element_type=F32)


def _bdot_nt(a, b):
    return lax.dot_general(a.astype(BF16), b.astype(BF16), (((1,), (1,)), ((), ())),
                           preferred_element_type=F32)


def _bdot_tn(a, b):
    return lax.dot_general(a.astype(BF16), b.astype(BF16), (((0,), (0,)), ((), ())),
                           preferred_element_type=F32)


def _hdot(a, b):
    return jnp.dot(a, b, preferred_element_type=F32, precision=lax.Precision.HIGHEST)


def _silu(x):
    return x * jax.nn.sigmoid(x)


def _rms(x):
    return x * lax.rsqrt(jnp.mean(x * x, axis=-1, keepdims=True) + NORM_EPS)


def _row_tile(n_lat, n_ctx, cap):
    for t in (1024, 512, 256):
        if t <= cap and n_lat % t == 0 and n_ctx % t == 0:
            return t
    raise ValueError("row counts must be multiples of 256")


def _ada_kernel(c_ref, w_ref, b_ref, o_ref):
    o_ref[...] = _bdot(_silu(c_ref[...]), w_ref[...]) + b_ref[...]


def ada_mod(cc, ada_w, ada_b):
    depth, d, n6 = ada_w.shape
    tn = 1024
    out = pl.pallas_call(
        _ada_kernel,
        grid=(depth, n6 // tn),
        in_specs=[
            pl.BlockSpec((MOD_ROWS, d), lambda l, j: (0, 0)),
            pl.BlockSpec((None, d, tn), lambda l, j: (l, 0, j)),
            pl.BlockSpec((None, 1, tn), lambda l, j: (l, 0, j)),
        ],
        out_specs=pl.BlockSpec((None, MOD_ROWS, tn), lambda l, j: (l, 0, j)),
        out_shape=jax.ShapeDtypeStruct((depth, MOD_ROWS, n6), F32),
        compiler_params=_cparams(("arbitrary", "arbitrary")),
        name="ada_mod",
    )(cc, ada_w, ada_b.reshape(depth, 1, n6))
    return out.reshape(depth, MOD_ROWS, 6, d)


def _norm_mod(x, g, shift, scale):
    return (_rms(x) * g) * (1.0 + scale) + shift


def _seg_fn(n_lat, seq, tm):
    n_lat_tiles = n_lat // tm
    n_batch = n_lat // seq
    return lambda i: jnp.where(i < n_lat_tiles, (i * tm) // seq, n_batch)


def _inproj_kernel(h_ref, mod_ref, g_ref, w_ref, *rest, has_aux):
    if has_aux:
        waux_ref, o_ref, oaux_ref, u_scr = rest
    else:
        o_ref, u_scr = rest

    @pl.when(pl.program_id(1) == 0)
    def _():
        u = _norm_mod(h_ref[...], g_ref[...], mod_ref[0:1, :], mod_ref[1:2, :])
        u_scr[...] = u.astype(BF16)
        if has_aux:
            oaux_ref[...] = jnp.dot(u_scr[...], waux_ref[...], preferred_element_type=F32)

    o_ref[...] = jnp.dot(u_scr[...], w_ref[...], preferred_element_type=F32)


def in_proj(hf, mod_l, g, w, w_aux, n_lat, seq, tn):
    m, d = hf.shape
    n = w.shape[1]
    tm = _row_tile(n_lat, m - n_lat, 1024)
    seg = _seg_fn(n_lat, seq, tm)
    has_aux = w_aux is not None
    in_specs = [
        pl.BlockSpec((tm, d), lambda i, j: (i, 0)),
        pl.BlockSpec((None, 6, d), lambda i, j: (seg(i), 0, 0)),
        pl.BlockSpec((1, d), lambda i, j: (0, 0)),
        pl.BlockSpec((d, tn), lambda i, j: (0, j)),
    ]
    out_specs = [pl.BlockSpec((tm, tn), lambda i, j: (i, j))]
    out_shape = [jax.ShapeDtypeStruct((m, n), F32)]
    args = [hf, mod_l, g.reshape(1, d), w]
    if has_aux:
        in_specs.append(pl.BlockSpec((d, HEAD_DIM), lambda i, j: (0, 0)))
        out_specs.append(pl.BlockSpec((tm, HEAD_DIM), lambda i, j: (i, 0)))
        out_shape.append(jax.ShapeDtypeStruct((m, HEAD_DIM), F32))
        args.append(w_aux)
    outs = pl.pallas_call(
        functools.partial(_inproj_kernel, has_aux=has_aux),
        grid=(m // tm, n // tn),
        in_specs=in_specs,
        out_specs=out_specs,
        out_shape=out_shape,
        scratch_shapes=[pltpu.VMEM((tm, d), BF16)],
        compiler_params=_cparams(("arbitrary", "arbitrary")),
        name="in_proj",
    )(*args)
    return outs if has_aux else outs[0]


def _outproj_kernel(*refs, n_lat_tiles, has_ctx):
    if has_ctx:
        yal_ref, ybl_ref, yac_ref, ybc_ref, h_ref, mod_ref, g_ref, w_ref, ho_ref, u_ref = refs
        is_ctx = pl.program_id(0) >= n_lat_tiles
        ya = jnp.where(is_ctx, yac_ref[...], yal_ref[...])
        yb = jnp.where(is_ctx, ybc_ref[...], ybl_ref[...])
    else:
        yal_ref, ybl_ref, h_ref, mod_ref, g_ref, w_ref, ho_ref, u_ref = refs
        ya, yb = yal_ref[...], ybl_ref[...]
    y = (jnp.dot(ya, w_ref[0:HALF_W, :], preferred_element_type=F32)
         + jnp.dot(yb, w_ref[HALF_W:2 * HALF_W, :], preferred_element_type=F32))
    hn = h_ref[...] + mod_ref[2:3, :] * y
    ho_ref[...] = hn
    u_ref[...] = _norm_mod(hn, g_ref[...], mod_ref[3:4, :], mod_ref[4:5, :]).astype(BF16)


def out_proj(y_lat, y_ctx, hf, mod_l, g, w, n_lat, seq):
    m, d = hf.shape
    has_ctx = y_ctx is not None
    m_out = m if has_ctx else n_lat
    tm = _row_tile(n_lat, m - n_lat, 512)
    n_lat_tiles = n_lat // tm
    seg = _seg_fn(n_lat, seq, tm)
    lat_spec = pl.BlockSpec((tm, HALF_W), lambda i: (jnp.minimum(i, n_lat_tiles - 1), 0))
    ctx_spec = pl.BlockSpec((tm, HALF_W), lambda i: (jnp.maximum(i - n_lat_tiles, 0), 0))
    in_specs = [lat_spec, lat_spec] + ([ctx_spec, ctx_spec] if has_ctx else []) + [
        pl.BlockSpec((tm, d), lambda i: (i, 0)),
        pl.BlockSpec((None, 6, d), lambda i: (seg(i), 0, 0)),
        pl.BlockSpec((1, d), lambda i: (0, 0)),
        pl.BlockSpec((d, d), lambda i: (0, 0)),
    ]
    args = list(y_lat) + (list(y_ctx) if has_ctx else []) + [hf, mod_l, g.reshape(1, d), w]
    return pl.pallas_call(
        functools.partial(_outproj_kernel, n_lat_tiles=n_lat_tiles, has_ctx=has_ctx),
        grid=(m_out // tm,),
        in_specs=in_specs,
        out_specs=[pl.BlockSpec((tm, d), lambda i: (i, 0)), pl.BlockSpec((tm, d), lambda i: (i, 0))],
        out_shape=[jax.ShapeDtypeStruct((m_out, d), F32), jax.ShapeDtypeStruct((m_out, d), BF16)],
        compiler_params=_cparams(("arbitrary",)),
        name="out_proj",
    )(*args)


def _ffn_kernel(u_ref, h_ref, mod_ref, wg_ref, wu_ref, wd_ref, fg_ref, o_ref, acc_ref, *, final):
    j = pl.program_id(1)
    u = u_ref[...]
    gate = jnp.dot(u, wg_ref[...], preferred_element_type=F32)
    up = jnp.dot(u, wu_ref[...], preferred_element_type=F32)
    act = (_silu(gate) * up).astype(BF16)
    part = jnp.dot(act, wd_ref[...], preferred_element_type=F32)

    @pl.when(j == 0)
    def _():
        acc_ref[...] = part

    @pl.when(j > 0)
    def _():
        acc_ref[...] += part

    @pl.when(j == pl.num_programs(1) - 1)
    def _():
        hn = h_ref[...] + mod_ref[5:6, :] * acc_ref[...]
        if final:
            hn = _rms(hn) * fg_ref[...]
        o_ref[...] = hn


def ffn(u2, h1, mod_l, wg, wu, wd, final_g, n_lat, seq, final):
    m, d = h1.shape
    hidden = wg.shape[1]
    tm = _row_tile(n_lat, m - n_lat, 512)
    th = 512
    seg = _seg_fn(n_lat, seq, tm)
    return pl.pallas_call(
        functools.partial(_ffn_kernel, final=final),
        grid=(m // tm, hidden // th),
        in_specs=[
            pl.BlockSpec((tm, d), lambda i, j: (i, 0)),
            pl.BlockSpec((tm, d), lambda i, j: (i, 0)),
            pl.BlockSpec((None, 6, d), lambda i, j: (seg(i), 0, 0)),
            pl.BlockSpec((d, th), lambda i, j: (0, j)),
            pl.BlockSpec((d, th), lambda i, j: (0, j)),
            pl.BlockSpec((th, d), lambda i, j: (j, 0)),
            pl.BlockSpec((1, d), lambda i, j: (0, 0)),
        ],
        out_specs=pl.BlockSpec((tm, d), lambda i, j: (i, 0)),
        out_shape=jax.ShapeDtypeStruct((m, d), F32),
        scratch_shapes=[pltpu.VMEM((tm, d), F32)],
        compiler_params=_cparams(("arbitrary", "arbitrary")),
        name="ffn",
    )(u2, h1, mod_l, wg, wu, wd, final_g.reshape(1, d))


def _rope_tables(seq, comp):
    pos = np.arange(seq)
    rows, cols = pos // GRID_W, pos % GRID_W
    n = comp // 2
    inv = (ROPE_BASE ** (-np.arange(0, n, 2, dtype=np.float32) / n)).astype(np.float32)
    ang_r = rows.astype(np.float32)[:, None] * inv[None, :]
    ang_c = cols.astype(np.float32)[:, None] * inv[None, :]
    ang = jnp.asarray(np.concatenate([ang_r, ang_r, ang_c, ang_c], axis=1))
    sign = np.concatenate([-np.ones(n // 2), np.ones(n // 2)] * 2).astype(np.float32)
    reps = HEAD_DIM // comp
    return jnp.tile(jnp.cos(ang), (1, reps)), jnp.tile(jnp.sin(ang) * jnp.asarray(sign)[None, :], (1, reps))


def _rope(x, cos, sin, half):
    lane = lax.broadcasted_iota(jnp.int32, x.shape, 1)
    first = (lane % (2 * half)) < half
    fwd = pltpu.roll(x, HEAD_DIM - half, 1)
    bwd = pltpu.roll(x, half, 1)
    return x * cos + jnp.where(first, fwd, bwd) * sin


def _shift_rows(x, up):
    t = x.shape[0]
    row = lax.broadcasted_iota(jnp.int32, x.shape, 0)
    if up:
        return jnp.where(row == t - 1, 0.0, pltpu.roll(x, t - 1, 0))
    return jnp.where(row == 0, 0.0, pltpu.roll(x, 1, 0))


def _conv_silu(x, w):
    y = _shift_rows(x, False) * w[0:1, :] + x * w[1:2, :] + _shift_rows(x, True) * w[2:3, :]
    return _silu(y)


def _l2n(x):
    return x * lax.rsqrt(jnp.sum(x * x, axis=-1, keepdims=True) + NORM_EPS)


def _softplus(x):
    return jnp.maximum(x, 0.0) + jnp.log1p(jnp.exp(-jnp.abs(x)))


TRI_BASE = 8


def _unit_tri_inverse(a):
    c = a.shape[0]
    row = lax.broadcasted_iota(jnp.int32, (c, c), 0)
    col = lax.broadcasted_iota(jnp.int32, (c, c), 1)
    same = lambda n: (row // n) == (col // n)
    x = jnp.where(same(TRI_BASE), -a, 0.0)
    t = (row == col).astype(F32) + x
    p = 2
    while p < TRI_BASE:
        x = _bdot(x, x)
        t = t + _bdot(t, x)
        p *= 2
    n = 2 * TRI_BASE
    while n <= c:
        a_off = jnp.where(same(n) & jnp.logical_not(same(n // 2)), a, 0.0)
        t = t - _bdot(_bdot(t, a_off), t)
        n *= 2
    return t


def _gdn_chunk(q, k, v, g, beta, s, rev):
    c = q.shape[0]
    row = lax.broadcasted_iota(jnp.int32, (c, c), 0)
    col = lax.broadcasted_iota(jnp.int32, (c, c), 1)
    tri = (row <= col) if rev else (row >= col)
    strict = (row < col) if rev else (row > col)
    eye = (row == col).astype(F32)
    gc = _hdot(tri.astype(F32), g)
    gcs = gc[:, 0:c]
    gc_row = _hdot(jnp.ones((c, c), F32), eye * gcs)
    decay = jnp.exp(jnp.where(tri, gcs - gc_row, -jnp.inf))
    kb = k * beta
    a_mat = jnp.where(strict, _bdot_nt(kb, k) * decay, 0.0)
    t_mat = _unit_tri_inverse(a_mat)
    eg = jnp.exp(gc)
    u = _bdot(t_mat, v * beta)
    w = _bdot(t_mat, kb * eg)
    qs = q * (HEAD_DIM ** -0.5)
    attn = jnp.where(tri, _bdot_nt(qs, k) * decay, 0.0)
    g_last = gc[0:1, :] if rev else gc[c - 1:c, :]
    kdec = k * jnp.exp(g_last - gc)
    v_new = u - _bdot(w, s)
    o = _bdot(qs * eg, s) + _bdot(attn, v_new)
    s_new = s * jnp.exp(g_last) + _bdot_tn(kdec, v_new)
    return o, s_new


def _gdn_kernel(*refs, seq, n_ctx, need_ctx):
    (ql_ref, kl_ref, vl_ref, zl_ref, abl_ref, qc_ref, kc_ref, vc_ref, zc_ref, abc_ref,
     cwq_ref, cwk_ref, cwv_ref, alog_ref, dtb_ref, gn_ref) = refs[:16]
    if need_ctx:
        ol_ref, oc_ref = refs[16:18]
        scr = refs[18:]
    else:
        ol_ref = refs[16]
        scr = refs[17:]
    q_s, k_s, v_s, gt_s, of_s, ob_s = scr
    hd = pl.program_id(1)
    c = GDN_CHUNK
    for (src_q, src_k, src_v, lo, n) in ((qc_ref, kc_ref, vc_ref, 0, n_ctx), (ql_ref, kl_ref, vl_ref, n_ctx, seq)):
        q_s[lo:lo + n, :] = _l2n(_conv_silu(src_q[...], cwq_ref[...]))
        k_s[lo:lo + n, :] = _l2n(_conv_silu(src_k[...], cwk_ref[...]))
        v_s[lo:lo + n, :] = _conv_silu(src_v[...], cwv_ref[...])
    lane_r = lax.broadcasted_iota(jnp.int32, (HEAD_DIM, HEAD_DIM), 0)
    for d in range(2):
        sel_a = (lane_r == d * N_HEADS_HALF + hd).astype(F32)
        sel_b = (lane_r == (2 + d) * N_HEADS_HALF + hd).astype(F32)
        neg_decay = -jnp.exp(jnp.full((1, HEAD_DIM), alog_ref[d, hd], F32))
        dt_b = jnp.full((1, HEAD_DIM), dtb_ref[d, hd], F32)
        for (src, lo, n) in ((abc_ref, 0, n_ctx), (abl_ref, n_ctx, seq)):
            ab = src[...]
            gt_s[2 * d, lo:lo + n, :] = neg_decay * _softplus(_hdot(ab, sel_a) + dt_b)
            gt_s[2 * d + 1, lo:lo + n, :] = jax.nn.sigmoid(_hdot(ab, sel_b))

    n_chunks = (n_ctx + seq) // c
    n_cc = n_ctx // c

    def body(t, carry):
        s_f, s_b = carry
        r_f = pl.multiple_of(t * c, c)
        cb = jnp.where(t < n_cc, n_cc - 1 - t, n_chunks - 1 + n_cc - t)
        r_b = pl.multiple_of(cb * c, c)
        o_f, s_f = _gdn_chunk(q_s[pl.ds(r_f, c), :], k_s[pl.ds(r_f, c), :], v_s[pl.ds(r_f, c), :],
                              gt_s[0, pl.ds(r_f, c), :], gt_s[1, pl.ds(r_f, c), :], s_f, False)
        o_b, s_b = _gdn_chunk(q_s[pl.ds(r_b, c), :], k_s[pl.ds(r_b, c), :], v_s[pl.ds(r_b, c), :],
                              gt_s[2, pl.ds(r_b, c), :], gt_s[3, pl.ds(r_b, c), :], s_b, True)
        of_s[pl.ds(r_f, c), :] = o_f
        ob_s[pl.ds(r_b, c), :] = o_b
        return s_f, s_b

    zero = jnp.zeros((HEAD_DIM, HEAD_DIM), F32)
    lax.fori_loop(0, n_chunks, body, (zero, zero))

    def finish(lo, n, z_ref, out_ref):
        o = of_s[lo:lo + n, :] + ob_s[lo:lo + n, :]
        out_ref[...] = (_rms(o) * gn_ref[...] * _silu(z_ref[...])).astype(out_ref.dtype)

    finish(n_ctx, seq, zl_ref, ol_ref)
    if need_ctx:
        finish(0, n_ctx, zc_ref, oc_ref)


def gdn_mixer(p, ab, conv_w, a_log, dt_bias, gdn_g, n_batch, seq, n_ctx, need_ctx):
    cb = (n_batch * seq) // n_ctx
    nh = N_HEADS_HALF

    def lat(col0):
        return pl.BlockSpec((seq, HEAD_DIM), lambda b, h: (b, col0 + h))

    def ctxs(col0):
        return pl.BlockSpec((n_ctx, HEAD_DIM), lambda b, h: (cb + b, col0 + h))

    in_specs = [lat(0), lat(nh), lat(2 * nh), lat(3 * nh),
                pl.BlockSpec((seq, HEAD_DIM), lambda b, h: (b, 0)),
                ctxs(0), ctxs(nh), ctxs(2 * nh), ctxs(3 * nh),
                pl.BlockSpec((n_ctx, HEAD_DIM), lambda b, h: (cb + b, 0)),
                pl.BlockSpec((3, HEAD_DIM), lambda b, h: (0, h)),
                pl.BlockSpec((3, HEAD_DIM), lambda b, h: (0, nh + h)),
                pl.BlockSpec((3, HEAD_DIM), lambda b, h: (0, 2 * nh + h)),
                pl.BlockSpec(memory_space=pltpu.SMEM),
                pl.BlockSpec(memory_space=pltpu.SMEM),
                pl.BlockSpec((1, HEAD_DIM), lambda b, h: (0, 0))]
    out_specs = [pl.BlockSpec((seq, HEAD_DIM), lambda b, h: (b, h))]
    out_shape = [jax.ShapeDtypeStruct((n_batch * seq, HALF_W), BF16)]
    if need_ctx:
        out_specs.append(pl.BlockSpec((n_ctx, HEAD_DIM), lambda b, h: (b, h)))
        out_shape.append(jax.ShapeDtypeStruct((n_batch * n_ctx, HALF_W), BF16))
    tot = seq + n_ctx
    outs = pl.pallas_call(
        functools.partial(_gdn_kernel, seq=seq, n_ctx=n_ctx, need_ctx=need_ctx),
        grid=(n_batch, nh),
        in_specs=in_specs,
        out_specs=out_specs,
        out_shape=out_shape,
        scratch_shapes=[pltpu.VMEM((tot, HEAD_DIM), F32), pltpu.VMEM((tot, HEAD_DIM), F32),
                        pltpu.VMEM((tot, HEAD_DIM), F32), pltpu.VMEM((4, tot, HEAD_DIM), F32),
                        pltpu.VMEM((tot, HEAD_DIM), F32), pltpu.VMEM((tot, HEAD_DIM), F32)],
        compiler_params=_cparams(("arbitrary", "arbitrary")),
        name="gdn",
    )(p, p, p, p, ab, p, p, p, p, ab, conv_w, conv_w, conv_w, a_log, dt_bias, gdn_g.reshape(1, HEAD_DIM))
    return (outs[0], outs[1]) if need_ctx else (outs[0], None)


def _softmax_pv(q, k, v_ext):
    s = _bdot_nt(q, k)
    e = jnp.exp(s - jnp.max(s, axis=-1, keepdims=True)).astype(BF16)
    r = jnp.dot(e, v_ext, preferred_element_type=F32)
    return r[:, 0:HEAD_DIM] / r[:, HEAD_DIM:2 * HEAD_DIM]


def _diff_kernel(*refs, seq, n_ctx, need_ctx, lambda_init):
    (q_ref, kl_ref, vl_ref, kc_ref, vc_ref, cq_ref, sq_ref, ck_ref, sk_ref,
     lq1_ref, lk1_ref, lq2_ref, lk2_ref, sg_ref) = refs[:14]
    if need_ctx:
        qc_ref, o_ref, oc_ref, k1_s, k2_s, v_s = refs[14:]
    else:
        o_ref, k1_s, k2_s, v_s = refs[14:]
    tot = seq + n_ctx
    qscale = DIFF_DH ** -0.5
    lam = (jnp.exp(jnp.sum(lq1_ref[...] * lk1_ref[...], axis=-1, keepdims=True))
           - jnp.exp(jnp.sum(lq2_ref[...] * lk2_ref[...], axis=-1, keepdims=True)) + lambda_init)

    def combine(qv, lo, hi):
        o1 = _softmax_pv(qv[:, 0:DIFF_DH], k1_s[lo:hi, :], v_s[lo:hi, :])
        o2 = _softmax_pv(qv[:, DIFF_DH:2 * DIFF_DH], k2_s[lo:hi, :], v_s[lo:hi, :])
        o = o1 - lam * o2
        return (_rms(o) * sg_ref[...] * (1.0 - lambda_init)).astype(BF16)

    @pl.when(pl.program_id(2) == 0)
    def _():
        k = _rope(kl_ref[...], ck_ref[...], sk_ref[...], DIFF_DH // 4)
        k1_s[0:seq, :] = k[:, 0:DIFF_DH].astype(BF16)
        k2_s[0:seq, :] = k[:, DIFF_DH:2 * DIFF_DH].astype(BF16)
        kc = kc_ref[...]
        k1_s[seq:tot, :] = kc[:, 0:DIFF_DH].astype(BF16)
        k2_s[seq:tot, :] = kc[:, DIFF_DH:2 * DIFF_DH].astype(BF16)
        v_s[0:seq, 0:HEAD_DIM] = vl_ref[...].astype(BF16)
        v_s[seq:tot, 0:HEAD_DIM] = vc_ref[...].astype(BF16)
        v_s[:, HEAD_DIM:2 * HEAD_DIM] = jnp.ones((tot, HEAD_DIM), BF16)
        if need_ctx:
            oc_ref[...] = combine(qc_ref[...] * qscale, seq, tot)

    q = _rope(q_ref[...], cq_ref[...], sq_ref[...], DIFF_DH // 4) * qscale
    o_ref[...] = combine(q, 0, tot)


def diff_mixer(p, cos, sin, lq1, lk1, lq2, lk2, subln_g, lambda_init, n_batch, seq, n_ctx, need_ctx):
    nh = N_HEADS_HALF
    tq = 256
    nq = seq // tq
    cb = (n_batch * seq) // n_ctx
    vec = lambda n: pl.BlockSpec((1, n), lambda b, h, i: (0, 0))
    in_specs = [
        pl.BlockSpec((tq, HEAD_DIM), lambda b, h, i: (b * nq + i, 4 * nh + h)),
        pl.BlockSpec((seq, HEAD_DIM), lambda b, h, i: (b, 5 * nh + h)),
        pl.BlockSpec((seq, HEAD_DIM), lambda b, h, i: (b, 6 * nh + h)),
        pl.BlockSpec((n_ctx, HEAD_DIM), lambda b, h, i: (cb + b, 5 * nh + h)),
        pl.BlockSpec((n_ctx, HEAD_DIM), lambda b, h, i: (cb + b, 6 * nh + h)),
        pl.BlockSpec((tq, HEAD_DIM), lambda b, h, i: (i, 0)),
        pl.BlockSpec((tq, HEAD_DIM), lambda b, h, i: (i, 0)),
        pl.BlockSpec((seq, HEAD_DIM), lambda b, h, i: (0, 0)),
        pl.BlockSpec((seq, HEAD_DIM), lambda b, h, i: (0, 0)),
        vec(DIFF_DH), vec(DIFF_DH), vec(DIFF_DH), vec(DIFF_DH), vec(HEAD_DIM),
    ]
    args = [p, p, p, p, p, cos, sin, cos, sin,
            lq1.reshape(1, -1), lk1.reshape(1, -1), lq2.reshape(1, -1), lk2.reshape(1, -1), subln_g.reshape(1, -1)]
    out_specs = [pl.BlockSpec((tq, HEAD_DIM), lambda b, h, i: (b * nq + i, h))]
    out_shape = [jax.ShapeDtypeStruct((n_batch * seq, HALF_W), BF16)]
    if need_ctx:
        in_specs.append(pl.BlockSpec((n_ctx, HEAD_DIM), lambda b, h, i: (cb + b, 4 * nh + h)))
        args.append(p)
        out_specs.append(pl.BlockSpec((n_ctx, HEAD_DIM), lambda b, h, i: (b, h)))
        out_shape.append(jax.ShapeDtypeStruct((n_batch * n_ctx, HALF_W), BF16))
    tot = seq + n_ctx
    outs = pl.pallas_call(
        functools.partial(_diff_kernel, seq=seq, n_ctx=n_ctx, need_ctx=need_ctx, lambda_init=lambda_init),
        grid=(n_batch, nh, nq),
        in_specs=in_specs,
        out_specs=out_specs,
        out_shape=out_shape,
        scratch_shapes=[pltpu.VMEM((tot, DIFF_DH), BF16), pltpu.VMEM((tot, DIFF_DH), BF16),
                        pltpu.VMEM((tot, 2 * HEAD_DIM), BF16)],
        compiler_params=_cparams(("arbitrary", "arbitrary", "arbitrary")),
        name="diff_attn",
    )(*args)
    return (outs[0], outs[1]) if need_ctx else (outs[0], None)


def _na_bias_index(n_rows):
    kr = min(NA_ROWS, n_rows)
    c = np.arange(GRID_W)[:, None, None]
    i = np.arange(kr)[None, :, None]
    kc = np.arange(GRID_W)[None, None, :]
    win_c = np.clip(c - NA_COLS // 2, 0, GRID_W - NA_COLS)
    ok = (kc >= win_c) & (kc < win_c + NA_COLS)
    dc = np.clip(kc - c + NA_COLS - 1, 0, 2 * NA_COLS - 2)
    o = np.arange(NA_ROWS)[:, None, None, None]
    dr = np.clip(i[None] - o + NA_ROWS - 1, 0, 2 * NA_ROWS - 2)
    shape = (NA_ROWS, GRID_W, kr, GRID_W)
    dr = np.broadcast_to(dr, shape).reshape(NA_ROWS, GRID_W, kr * GRID_W)
    dc = np.broadcast_to(dc[None], shape).reshape(NA_ROWS, GRID_W, kr * GRID_W)
    ok = np.broadcast_to(ok[None] & np.ones((1, 1, kr, 1), bool), shape).reshape(NA_ROWS, GRID_W, kr * GRID_W)
    return dr, dc, ok


def _na_kernel(*refs, seq, n_ctx, need_ctx):
    q_ref, k_ref, v_ref, kc_ref, vc_ref, bias_ref = refs[:6]
    if need_ctx:
        qc_ref, o_ref, oc_ref, k_s, v_s, kc_s, vc_s = refs[6:]
    else:
        o_ref, k_s, v_s, kc_s, vc_s = refs[6:]
    n_rows = seq // GRID_W
    kr = min(NA_ROWS, n_rows)
    nl = kr * GRID_W
    scale = HEAD_DIM ** -0.5
    k_s[...] = k_ref[...].astype(BF16)
    v_s[...] = v_ref[...].astype(BF16)
    kc_s[...] = kc_ref[...].astype(BF16)
    vc_s[...] = vc_ref[...].astype(BF16)

    def row_block(r, carry):
        start = jnp.clip(r - kr // 2, 0, n_rows - kr)
        q = q_ref[pl.ds(pl.multiple_of(r * GRID_W, GRID_W), GRID_W), :]
        k0 = pl.multiple_of(start * GRID_W, GRID_W)
        s_loc = _bdot_nt(q, k_s[pl.ds(k0, nl), :]) * scale + bias_ref[r - start]
        s_ctx = _bdot_nt(q, kc_s[...]) * scale
        m = jnp.maximum(jnp.max(s_loc, axis=-1, keepdims=True), jnp.max(s_ctx, axis=-1, keepdims=True))
        e_loc = jnp.exp(s_loc - m)
        e_ctx = jnp.exp(s_ctx - m)
        den = jnp.sum(e_loc, axis=-1, keepdims=True) + jnp.sum(e_ctx, axis=-1, keepdims=True)
        o = _bdot(e_loc, v_s[pl.ds(k0, nl), :]) + _bdot(e_ctx, vc_s[...])
        o_ref[pl.ds(pl.multiple_of(r * GRID_W, GRID_W), GRID_W), :] = (o / den).astype(BF16)
        return carry

    lax.fori_loop(0, n_rows, row_block, 0)
    if need_ctx:
        s = _bdot_nt(qc_ref[...], kc_s[...]) * scale
        e = jnp.exp(s - jnp.max(s, axis=-1, keepdims=True))
        oc_ref[...] = (_bdot(e, vc_s[...]) / jnp.sum(e, axis=-1, keepdims=True)).astype(BF16)


def na_mixer(p, bias_tab, n_batch, seq, n_ctx, need_ctx):
    nh = N_HEADS_HALF
    cb = (n_batch * seq) // n_ctx
    nl = bias_tab.shape[-1]
    in_specs = [
        pl.BlockSpec((seq, HEAD_DIM), lambda b, h: (b, h)),
        pl.BlockSpec((seq, HEAD_DIM), lambda b, h: (b, nh + h)),
        pl.BlockSpec((seq, HEAD_DIM), lambda b, h: (b, 2 * nh + h)),
        pl.BlockSpec((n_ctx, HEAD_DIM), lambda b, h: (cb + b, nh + h)),
        pl.BlockSpec((n_ctx, HEAD_DIM), lambda b, h: (cb + b, 2 * nh + h)),
        pl.BlockSpec((None, NA_ROWS, GRID_W, nl), lambda b, h: (h, 0, 0, 0)),
    ]
    args = [p, p, p, p, p, bias_tab]
    out_specs = [pl.BlockSpec((seq, HEAD_DIM), lambda b, h: (b, h))]
    out_shape = [jax.ShapeDtypeStruct((n_batch * seq, HALF_W), BF16)]
    if need_ctx:
        in_specs.append(pl.BlockSpec((n_ctx, HEAD_DIM), lambda b, h: (cb + b, h)))
        args.append(p)
        out_specs.append(pl.BlockSpec((n_ctx, HEAD_DIM), lambda b, h: (b, h)))
        out_shape.append(jax.ShapeDtypeStruct((n_batch * n_ctx, HALF_W), BF16))
    outs = pl.pallas_call(
        functools.partial(_na_kernel, seq=seq, n_ctx=n_ctx, need_ctx=need_ctx),
        grid=(n_batch, nh),
        in_specs=in_specs,
        out_specs=out_specs,
        out_shape=out_shape,
        scratch_shapes=[pltpu.VMEM((seq, HEAD_DIM), BF16), pltpu.VMEM((seq, HEAD_DIM), BF16),
                        pltpu.VMEM((n_ctx, HEAD_DIM), BF16), pltpu.VMEM((n_ctx, HEAD_DIM), BF16)],
        compiler_params=_cparams(("arbitrary", "arbitrary")),
        name="na_attn",
    )(*args)
    return (outs[0], outs[1]) if need_ctx else (outs[0], None)


def _win_kernel(*refs, seq, n_ctx, need_ctx):
    q_ref, k_ref, v_ref, kc_ref, vc_ref, cos_ref, sin_ref, sink_ref = refs[:8]
    if need_ctx:
        qc_ref, o_ref, oc_ref, k_s, v_s, kc_s, vc_s = refs[8:]
    else:
        o_ref, k_s, v_s, kc_s, vc_s = refs[8:]
    kvh = pl.program_id(1)
    wb = WIN_BLOCK
    band = 3 * wb
    scale = HEAD_DIM ** -0.5
    k_s[...] = _rope(k_ref[...], cos_ref[...], sin_ref[...], HEAD_DIM // 4).astype(BF16)
    v_s[...] = v_ref[...].astype(BF16)
    kc_s[...] = kc_ref[...].astype(BF16)
    vc_s[...] = vc_ref[...].astype(BF16)
    sinks = [jnp.full((1, 1), sink_ref[kvh * WIN_GROUP + j], F32) for j in range(WIN_GROUP)]

    def attend(q, j, k_loc, v_loc, valid):
        s_ctx = _bdot_nt(q, kc_s[...]) * scale
        m = jnp.maximum(jnp.max(s_ctx, axis=-1, keepdims=True), sinks[j])
        if k_loc is not None:
            s_loc = jnp.where(valid, _bdot_nt(q, k_loc) * scale, NEG_BIG)
            m = jnp.maximum(m, jnp.max(s_loc, axis=-1, keepdims=True))
        e_ctx = jnp.exp(s_ctx - m)
        den = jnp.sum(e_ctx, axis=-1, keepdims=True) + jnp.exp(sinks[j] - m)
        o = _bdot(e_ctx, vc_s[...])
        if k_loc is not None:
            e_loc = jnp.exp(s_loc - m)
            den = den + jnp.sum(e_loc, axis=-1, keepdims=True)
            o = o + _bdot(e_loc, v_loc)
        return (o / den).astype(BF16)

    def q_block(n, carry):
        q0 = pl.multiple_of(n * wb, wb)
        k0 = pl.multiple_of(jnp.clip(n * wb - wb, 0, seq - band), wb)
        qpos = q0 + lax.broadcasted_iota(jnp.int32, (wb, band), 0)
        kpos = k0 + lax.broadcasted_iota(jnp.int32, (wb, band), 1)
        valid = jnp.abs(qpos - kpos) <= WIN
        k_loc = k_s[pl.ds(k0, band), :]
        v_loc = v_s[pl.ds(k0, band), :]
        cos = cos_ref[pl.ds(q0, wb), :]
        sin = sin_ref[pl.ds(q0, wb), :]
        for j in range(WIN_GROUP):
            q = _rope(q_ref[pl.ds(q0, wb), j * HEAD_DIM:(j + 1) * HEAD_DIM], cos, sin, HEAD_DIM // 4)
            o_ref[pl.ds(q0, wb), j * HEAD_DIM:(j + 1) * HEAD_DIM] = attend(q, j, k_loc, v_loc, valid)
        return carry

    lax.fori_loop(0, seq // wb, q_block, 0)
    if need_ctx:
        for j in range(WIN_GROUP):
            q = qc_ref[:, j * HEAD_DIM:(j + 1) * HEAD_DIM]
            oc_ref[:, j * HEAD_DIM:(j + 1) * HEAD_DIM] = attend(q, j, None, None, None)


def win_mixer(p, cos, sin, sink, n_batch, seq, n_ctx, need_ctx):
    nh = N_HEADS_HALF
    cb = (n_batch * seq) // n_ctx
    gw = WIN_GROUP * HEAD_DIM
    q_blk0 = (3 * nh * HEAD_DIM) // gw
    in_specs = [
        pl.BlockSpec((seq, gw), lambda b, g: (b, q_blk0 + g)),
        pl.BlockSpec((seq, HEAD_DIM), lambda b, g: (b, 4 * nh + g)),
        pl.BlockSpec((seq, HEAD_DIM), lambda b, g: (b, 4 * nh + WIN_KV_HEADS + g)),
        pl.BlockSpec((n_ctx, HEAD_DIM), lambda b, g: (cb + b, 4 * nh + g)),
        pl.BlockSpec((n_ctx, HEAD_DIM), lambda b, g: (cb + b, 4 * nh + WIN_KV_HEADS + g)),
        pl.BlockSpec((seq, HEAD_DIM), lambda b, g: (0, 0)),
        pl.BlockSpec((seq, HEAD_DIM), lambda b, g: (0, 0)),
        pl.BlockSpec(memory_space=pltpu.SMEM),
    ]
    args = [p, p, p, p, p, cos, sin, sink]
    out_specs = [pl.BlockSpec((seq, gw), lambda b, g: (b, g))]
    out_shape = [jax.ShapeDtypeStruct((n_batch * seq, HALF_W), BF16)]
    if need_ctx:
        in_specs.append(pl.BlockSpec((n_ctx, gw), lambda b, g: (cb + b, q_blk0 + g)))
        args.append(p)
        out_specs.append(pl.BlockSpec((n_ctx, gw), lambda b, g: (b, g)))
        out_shape.append(jax.ShapeDtypeStruct((n_batch * n_ctx, HALF_W), BF16))
    outs = pl.pallas_call(
        functools.partial(_win_kernel, seq=seq, n_ctx=n_ctx, need_ctx=need_ctx),
        grid=(n_batch, WIN_KV_HEADS),
        in_specs=in_specs,
        out_specs=out_specs,
        out_shape=out_shape,
        scratch_shapes=[pltpu.VMEM((seq, HEAD_DIM), BF16), pltpu.VMEM((seq, HEAD_DIM), BF16),
                        pltpu.VMEM((n_ctx, HEAD_DIM), BF16), pltpu.VMEM((n_ctx, HEAD_DIM), BF16)],
        compiler_params=_cparams(("arbitrary", "arbitrary")),
        name="win_attn",
    )(*args)
    return (outs[0], outs[1]) if need_ctx else (outs[0], None)


def kernel(x, c, ctx, c_ctx, ada_w, ada_b, norm_mix_g, norm_ffn_g, w_in_even, gdn_conv_w, gdn_a_log, gdn_dt_bias,
           gdn_norm_g, diff_lambda_q1, diff_lambda_k1, diff_lambda_q2, diff_lambda_k2, diff_subln_g, w_in_odd,
           na_rpb, win_sink, w_out, ffn_w_gate, ffn_w_up, ffn_w_down, final_norm_g):
    n_batch, seq, d = x.shape
    n_ctx = ctx.shape[1]
    n_lat = n_batch * seq
    assert n_batch + 1 <= MOD_ROWS and seq % GRID_W == 0
    hf = jnp.concatenate([x.reshape(n_lat, d), ctx.reshape(n_batch * n_ctx, d)], axis=0)
    cc = jnp.zeros((MOD_ROWS, d), F32).at[:n_batch].set(c).at[n_batch].set(c_ctx)
    mods = ada_mod(cc, ada_w, ada_b)

    cos64, sin64 = _rope_tables(seq, DIFF_DH)
    cos128, sin128 = _rope_tables(seq, HEAD_DIM)
    dr_idx, dc_idx, ok = _na_bias_index(seq // GRID_W)

    qkvz = 4 * HALF_W
    for l in range(DEPTH):
        need_ctx = l < DEPTH - 1
        i = l // 2
        if l % 2 == 0:
            w = w_in_even[i]
            w_main = jnp.concatenate([w[:, :qkvz], w[:, qkvz + 4 * N_HEADS_HALF:]], axis=1).astype(BF16)
            w_ab = jnp.pad(w[:, qkvz:qkvz + 4 * N_HEADS_HALF], ((0, 0), (0, HEAD_DIM - 4 * N_HEADS_HALF))).astype(BF16)
            p, ab = in_proj(hf, mods[l], norm_mix_g[l], w_main, w_ab, n_lat, seq, 1024)
            lambda_init = 0.8 - 0.6 * float(np.exp(-0.3 * l))
            ya = gdn_mixer(p, ab, gdn_conv_w[i], gdn_a_log[i], gdn_dt_bias[i], gdn_norm_g[i],
                           n_batch, seq, n_ctx, need_ctx)
            yb = diff_mixer(p, cos64, sin64, diff_lambda_q1[i], diff_lambda_k1[i], diff_lambda_q2[i],
                            diff_lambda_k2[i], diff_subln_g[i], lambda_init, n_batch, seq, n_ctx, need_ctx)
        else:
            p = in_proj(hf, mods[l], norm_mix_g[l], w_in_odd[i].astype(BF16), None, n_lat, seq, 1536)
            bias_tab = jnp.where(ok[None], na_rpb[i][:, dr_idx, dc_idx], NEG_BIG)
            ya = na_mixer(p, bias_tab, n_batch, seq, n_ctx, need_ctx)
            yb = win_mixer(p, cos128, sin128, win_sink[i], n_batch, seq, n_ctx, need_ctx)
        y_ctx = (ya[1], yb[1]) if need_ctx else None
        h1, u2 = out_proj((ya[0], yb[0]), y_ctx, hf, mods[l], norm_ffn_g[l], w_out[l].astype(BF16), n_lat, seq)
        hf = ffn(u2, h1, mods[l], ffn_w_gate[l].astype(BF16), ffn_w_up[l].astype(BF16), ffn_w_down[l].astype(BF16),
                 final_norm_g, n_lat, seq, final=not need_ctx)
    return hf[:n_lat].reshape(n_batch, seq, d)
```

```python
import functools

import jax
import jax.numpy as jnp
import numpy as np
from jax import lax
from jax.experimental import pallas as pl
from jax.experimental.pallas import tpu as pltpu

F32 = jnp.float32
BF16 = jnp.bfloat16

DEPTH = 4
GRID_W = 64
HEAD_DIM = 128
N_HEADS_HALF = 8
HALF_W = N_HEADS_HALF * HEAD_DIM
GDN_CHUNK = 64
DIFF_DH = 64
NA_ROWS = 8
NA_COLS = 16
NA_ROW_GROUP = 4
WIN = 128
WIN_BLOCK = 128
WIN_KV_HEADS = 2
WIN_GROUP = N_HEADS_HALF // WIN_KV_HEADS
ROPE_BASE = 10000.0
NORM_EPS = 1e-6
LOG2E = 1.4426950408889634
NEG_BIG = -1e30

V7X_VMEM_LIMIT = 56 * 1024 * 1024
MOD_ROWS = 16


def _cparams(sem):
    return pltpu.CompilerParams(dimension_semantics=sem, vmem_limit_bytes=V7X_VMEM_LIMIT)


def _bdot(a, b):
    return jnp.dot(a.astype(BF16), b.astype(BF16), preferred_element_type=F32)


def _bdot_nt(a, b):
    return lax.dot_general(a.astype(BF16), b.astype(BF16), (((1,), (1,)), ((), ())),
                           preferred_element_type=F32)


def _bdot_tn(a, b):
    return lax.dot_general(a.astype(BF16), b.astype(BF16), (((0,), (0,)), ((), ())),
                           preferred_element_type=F32)


def _hdot(a, b):
    return jnp.dot(a, b, preferred_element_type=F32, precision=lax.Precision.HIGHEST)


def _silu(x):
    return x * jax.nn.sigmoid(x)


def _rms(x):
    return x * lax.rsqrt(jnp.mean(x * x, axis=-1, keepdims=True) + NORM_EPS)


def _row_tile(n_lat, n_ctx, cap):
    for t in (1024, 512, 256):
        if t <= cap and n_lat % t == 0 and n_ctx % t == 0:
            return t
    raise ValueError("row counts must be multiples of 256")


def _ada_kernel(c_ref, w_ref, b_ref, o_ref):
    o_ref[...] = _bdot(_silu(c_ref[...]), w_ref[...]) + b_ref[...]


def ada_mod(cc, ada_w, ada_b):
    depth, d, n6 = ada_w.shape
    tn = 1024
    out = pl.pallas_call(
        _ada_kernel,
        grid=(depth, n6 // tn),
        in_specs=[
            pl.BlockSpec((MOD_ROWS, d), lambda l, j: (0, 0)),
            pl.BlockSpec((None, d, tn), lambda l, j: (l, 0, j)),
            pl.BlockSpec((None, 1, tn), lambda l, j: (l, 0, j)),
        ],
        out_specs=pl.BlockSpec((None, MOD_ROWS, tn), lambda l, j: (l, 0, j)),
        out_shape=jax.ShapeDtypeStruct((depth, MOD_ROWS, n6), F32),
        compiler_params=_cparams(("arbitrary", "arbitrary")),
        name="ada_mod",
    )(cc, ada_w, ada_b.reshape(depth, 1, n6))
    return out.reshape(depth, MOD_ROWS, 6, d)


def _norm_mod(x, g, shift, scale):
    return (_rms(x) * g) * (1.0 + scale) + shift


def _seg_fn(n_lat, seq, tm):
    n_lat_tiles = n_lat // tm
    n_batch = n_lat // seq
    return lambda i: jnp.where(i < n_lat_tiles, (i * tm) // seq, n_batch)


def _inproj_kernel(h_ref, mod_ref, g_ref, w_ref, *rest, has_aux):
    if has_aux:
        waux_ref, o_ref, oaux_ref, u_scr = rest
    else:
        o_ref, u_scr = rest

    @pl.when(pl.program_id(1) == 0)
    def _():
        u = _norm_mod(h_ref[...], g_ref[...], mod_ref[0:1, :], mod_ref[1:2, :])
        u_scr[...] = u.astype(BF16)
        if has_aux:
            oaux_ref[...] = jnp.dot(u_scr[...], waux_ref[...], preferred_element_type=F32)

    o_ref[...] = jnp.dot(u_scr[...], w_ref[...], preferred_element_type=F32)


def in_proj(hf, mod_l, g, w, w_aux, n_lat, seq, tn):
    m, d = hf.shape
    n = w.shape[1]
    tm = _row_tile(n_lat, m - n_lat, 1024)
    seg = _seg_fn(n_lat, seq, tm)
    has_aux = w_aux is not None
    in_specs = [
        pl.BlockSpec((tm, d), lambda i, j: (i, 0)),
        pl.BlockSpec((None, 6, d), lambda i, j: (seg(i), 0, 0)),
        pl.BlockSpec((1, d), lambda i, j: (0, 0)),
        pl.BlockSpec((d, tn), lambda i, j: (0, j)),
    ]
    out_specs = [pl.BlockSpec((tm, tn), lambda i, j: (i, j))]
    out_shape = [jax.ShapeDtypeStruct((m, n), F32)]
    args = [hf, mod_l, g.reshape(1, d), w]
    if has_aux:
        in_specs.append(pl.BlockSpec((d, HEAD_DIM), lambda i, j: (0, 0)))
        out_specs.append(pl.BlockSpec((tm, HEAD_DIM), lambda i, j: (i, 0)))
        out_shape.append(jax.ShapeDtypeStruct((m, HEAD_DIM), F32))
        args.append(w_aux)
    outs = pl.pallas_call(
        functools.partial(_inproj_kernel, has_aux=has_aux),
        grid=(m // tm, n // tn),
        in_specs=in_specs,
        out_specs=out_specs,
        out_shape=out_shape,
        scratch_shapes=[pltpu.VMEM((tm, d), BF16)],
        compiler_params=_cparams(("arbitrary", "arbitrary")),
        name="in_proj",
    )(*args)
    return outs if has_aux else outs[0]


def _outproj_kernel(*refs, n_lat_tiles, has_ctx):
    if has_ctx:
        yal_ref, ybl_ref, yac_ref, ybc_ref, h_ref, mod_ref, g_ref, w_ref, ho_ref, u_ref = refs
        is_ctx = pl.program_id(0) >= n_lat_tiles
        ya = jnp.where(is_ctx, yac_ref[...], yal_ref[...])
        yb = jnp.where(is_ctx, ybc_ref[...], ybl_ref[...])
    else:
        yal_ref, ybl_ref, h_ref, mod_ref, g_ref, w_ref, ho_ref, u_ref = refs
        ya, yb = yal_ref[...], ybl_ref[...]
    y = (jnp.dot(ya, w_ref[0:HALF_W, :], preferred_element_type=F32)
         + jnp.dot(yb, w_ref[HALF_W:2 * HALF_W, :], preferred_element_type=F32))
    hn = h_ref[...] + mod_ref[2:3, :] * y
    ho_ref[...] = hn
    u_ref[...] = _norm_mod(hn, g_ref[...], mod_ref[3:4, :], mod_ref[4:5, :]).astype(BF16)


def out_proj(y_lat, y_ctx, hf, mod_l, g, w, n_lat, seq):
    m, d = hf.shape
    has_ctx = y_ctx is not None
    m_out = m if has_ctx else n_lat
    tm = _row_tile(n_lat, m - n_lat, 512)
    n_lat_tiles = n_lat // tm
    seg = _seg_fn(n_lat, seq, tm)
    lat_spec = pl.BlockSpec((tm, HALF_W), lambda i: (jnp.minimum(i, n_lat_tiles - 1), 0))
    ctx_spec = pl.BlockSpec((tm, HALF_W), lambda i: (jnp.maximum(i - n_lat_tiles, 0), 0))
    in_specs = [lat_spec, lat_spec] + ([ctx_spec, ctx_spec] if has_ctx else []) + [
        pl.BlockSpec((tm, d), lambda i: (i, 0)),
        pl.BlockSpec((None, 6, d), lambda i: (seg(i), 0, 0)),
        pl.BlockSpec((1, d), lambda i: (0, 0)),
        pl.BlockSpec((d, d), lambda i: (0, 0)),
    ]
    args = list(y_lat) + (list(y_ctx) if has_ctx else []) + [hf, mod_l, g.reshape(1, d), w]
    return pl.pallas_call(
        functools.partial(_outproj_kernel, n_lat_tiles=n_lat_tiles, has_ctx=has_ctx),
        grid=(m_out // tm,),
        in_specs=in_specs,
        out_specs=[pl.BlockSpec((tm, d), lambda i: (i, 0)), pl.BlockSpec((tm, d), lambda i: (i, 0))],
        out_shape=[jax.ShapeDtypeStruct((m_out, d), F32), jax.ShapeDtypeStruct((m_out, d), BF16)],
        compiler_params=_cparams(("arbitrary",)),
        name="out_proj",
    )(*args)


def _ffn_kernel(u_ref, h_ref, mod_ref, wg_ref, wu_ref, wd_ref, fg_ref, o_ref, acc_ref, *, final):
    j = pl.program_id(1)
    u = u_ref[...]
    gate = jnp.dot(u, wg_ref[...], preferred_element_type=F32)
    up = jnp.dot(u, wu_ref[...], preferred_element_type=F32)
    act = (_silu(gate) * up).astype(BF16)
    part = jnp.dot(act, wd_ref[...], preferred_element_type=F32)

    @pl.when(j == 0)
    def _():
        acc_ref[...] = part

    @pl.when(j > 0)
    def _():
        acc_ref[...] += part

    @pl.when(j == pl.num_programs(1) - 1)
    def _():
        hn = h_ref[...] + mod_ref[5:6, :] * acc_ref[...]
        if final:
            hn = _rms(hn) * fg_ref[...]
        o_ref[...] = hn


def ffn(u2, h1, mod_l, wg, wu, wd, final_g, n_lat, seq, final):
    m, d = h1.shape
    hidden = wg.shape[1]
    tm = _row_tile(n_lat, m - n_lat, 512)
    th = 512
    seg = _seg_fn(n_lat, seq, tm)
    return pl.pallas_call(
        functools.partial(_ffn_kernel, final=final),
        grid=(m // tm, hidden // th),
        in_specs=[
            pl.BlockSpec((tm, d), lambda i, j: (i, 0)),
            pl.BlockSpec((tm, d), lambda i, j: (i, 0)),
            pl.BlockSpec((None, 6, d), lambda i, j: (seg(i), 0, 0)),
            pl.BlockSpec((d, th), lambda i, j: (0, j)),
            pl.BlockSpec((d, th), lambda i, j: (0, j)),
            pl.BlockSpec((th, d), lambda i, j: (j, 0)),
            pl.BlockSpec((1, d), lambda i, j: (0, 0)),
        ],
        out_specs=pl.BlockSpec((tm, d), lambda i, j: (i, 0)),
        out_shape=jax.ShapeDtypeStruct((m, d), F32),
        scratch_shapes=[pltpu.VMEM((tm, d), F32)],
        compiler_params=_cparams(("arbitrary", "arbitrary")),
        name="ffn",
    )(u2, h1, mod_l, wg, wu, wd, final_g.reshape(1, d))


def _rope_tables(seq, comp):
    pos = np.arange(seq)
    rows, cols = pos // GRID_W, pos % GRID_W
    n = comp // 2
    inv = (ROPE_BASE ** (-np.arange(0, n, 2, dtype=np.float32) / n)).astype(np.float32)
    ang_r = rows.astype(np.float32)[:, None] * inv[None, :]
    ang_c = cols.astype(np.float32)[:, None] * inv[None, :]
    ang = jnp.asarray(np.concatenate([ang_r, ang_r, ang_c, ang_c], axis=1))
    sign = np.concatenate([-np.ones(n // 2), np.ones(n // 2)] * 2).astype(np.float32)
    reps = HEAD_DIM // comp
    return jnp.tile(jnp.cos(ang), (1, reps)), jnp.tile(jnp.sin(ang) * jnp.asarray(sign)[None, :], (1, reps))


def _rope(x, cos, sin, half):
    lane = lax.broadcasted_iota(jnp.int32, x.shape, 1)
    first = (lane % (2 * half)) < half
    fwd = pltpu.roll(x, HEAD_DIM - half, 1)
    bwd = pltpu.roll(x, half, 1)
    return x * cos + jnp.where(first, fwd, bwd) * sin


def _shift_rows(x, up):
    t = x.shape[0]
    row = lax.broadcasted_iota(jnp.int32, x.shape, 0)
    if up:
        return jnp.where(row == t - 1, 0.0, pltpu.roll(x, t - 1, 0))
    return jnp.where(row == 0, 0.0, pltpu.roll(x, 1, 0))


def _conv_silu(x, w):
    y = _shift_rows(x, False) * w[0:1, :] + x * w[1:2, :] + _shift_rows(x, True) * w[2:3, :]
    return _silu(y)


def _l2n(x):
    return x * lax.rsqrt(jnp.sum(x * x, axis=-1, keepdims=True) + NORM_EPS)


def _softplus(x):
    return jnp.maximum(x, 0.0) + jnp.log1p(jnp.exp(-jnp.abs(x)))


TRI_BASE = 8
GDN_PREP_GROUPS = (6, 4, 2, 1)


def _each(fn, *lists):
    return [fn(*xs) for xs in zip(*lists)]


def _unit_tri_inverse(mats):
    c = mats[0].shape[0]
    row = lax.broadcasted_iota(jnp.int32, (c, c), 0)
    col = lax.broadcasted_iota(jnp.int32, (c, c), 1)
    same = lambda n: (row // n) == (col // n)
    eye = (row == col).astype(F32)
    xs = [jnp.where(same(TRI_BASE), -a, 0.0) for a in mats]
    ts = [eye + x for x in xs]
    p = 2
    while p < TRI_BASE:
        xs = [_bdot(x, x) for x in xs]
        ts = _each(lambda t, x: t + _bdot(t, x), ts, xs)
        p *= 2
    n = 2 * TRI_BASE
    while n <= c:
        off = same(n) & jnp.logical_not(same(n // 2))
        tas = _each(lambda t, a: _bdot(t, jnp.where(off, a, 0.0)), ts, mats)
        ts = _each(lambda t, ta: t - _bdot(ta, t), ts, tas)
        n *= 2
    return ts


def _seg_cumsum(x, rev):
    t = x.shape[0]
    pos = lax.broadcasted_iota(jnp.int32, x.shape, 0) % GDN_CHUNK
    s = 1
    while s < GDN_CHUNK:
        if rev:
            x = x + jnp.where(pos < GDN_CHUNK - s, pltpu.roll(x, t - s, 0), 0.0)
        else:
            x = x + jnp.where(pos >= s, pltpu.roll(x, s, 0), 0.0)
        s *= 2
    return x


def _gdn_prep(qs, ks, vs, gcs, betas, revs):
    c = qs[0].shape[0]
    row = lax.broadcasted_iota(jnp.int32, (c, c), 0)
    col = lax.broadcasted_iota(jnp.int32, (c, c), 1)
    tri = {False: row >= col, True: row <= col}
    strict = {False: row > col, True: row < col}
    decays = _each(lambda gc, rev: jnp.exp(jnp.where(tri[rev], gc[:, 0:c] - jnp.transpose(gc)[0:c, :], -jnp.inf)),
                   gcs, revs)
    kbs = _each(lambda k, b: k * b, ks, betas)
    kks = _each(_bdot_nt, kbs, ks)
    qsc = [q * (HEAD_DIM ** -0.5) for q in qs]
    qks = _each(_bdot_nt, qsc, ks)
    a_mats = _each(lambda kk, dec, rev: jnp.where(strict[rev], kk * dec, 0.0), kks, decays, revs)
    t_mats = _unit_tri_inverse(a_mats)
    egs = [jnp.exp(gc) for gc in gcs]
    rhs = _each(lambda v, b, kb, eg: jnp.concatenate([v * b, kb * eg], axis=1), vs, betas, kbs, egs)
    uws = _each(_bdot, t_mats, rhs)
    attns = _each(lambda qk, dec, rev: jnp.where(tri[rev], qk * dec, 0.0).astype(BF16), qks, decays, revs)
    g_lasts = _each(lambda gc, rev: gc[0:1, :] if rev else gc[c - 1:c, :], gcs, revs)
    kdts = _each(lambda k, gl, gc: jnp.transpose(k * jnp.exp(gl - gc)).astype(BF16), ks, g_lasts, gcs)
    return [(uw[:, 0:HEAD_DIM], uw[:, HEAD_DIM:].astype(BF16), (q * eg).astype(BF16), attn, kdt, jnp.exp(gl))
            for uw, q, eg, attn, kdt, gl in zip(uws, qsc, egs, attns, kdts, g_lasts)]


def _gdn_kernel(*refs, seq, n_ctx, need_ctx):
    (ql_ref, kl_ref, vl_ref, zl_ref, abl_ref, qc_ref, kc_ref, vc_ref, zc_ref, abc_ref,
     cwq_ref, cwk_ref, cwv_ref, alog_ref, dtb_ref, gn_ref) = refs[:16]
    if need_ctx:
        ol_ref, oc_ref = refs[16:18]
        scr = refs[18:]
    else:
        ol_ref = refs[16]
        scr = refs[17:]
    q_s, k_s, v_s, gc_s, bt_s, u_s, o_s, wq_s, at_s, kd_s, gl_s = scr
    hd = pl.program_id(1)
    c = GDN_CHUNK
    for (src_q, src_k, src_v, lo, n) in ((qc_ref, kc_ref, vc_ref, 0, n_ctx), (ql_ref, kl_ref, vl_ref, n_ctx, seq)):
        q_s[lo:lo + n, :] = _l2n(_conv_silu(src_q[...], cwq_ref[...]))
        k_s[lo:lo + n, :] = _l2n(_conv_silu(src_k[...], cwk_ref[...]))
        v_s[lo:lo + n, :] = _conv_silu(src_v[...], cwv_ref[...])
    for d in range(2):
        neg_decay = -jnp.exp(jnp.full((1, HEAD_DIM), alog_ref[d, hd], F32))
        dt_b = jnp.full((1, HEAD_DIM), dtb_ref[d, hd], F32)
        for (src, lo, n) in ((abc_ref, 0, n_ctx), (abl_ref, n_ctx, seq)):
            a = jnp.broadcast_to(src[:, d:d + 1], (n, HEAD_DIM))
            b = jnp.broadcast_to(src[:, 2 + d:3 + d], (n, HEAD_DIM))
            gc_s[d, lo:lo + n, :] = _seg_cumsum(neg_decay * _softplus(a + dt_b), d == 1)
            bt_s[d, lo:lo + n, :] = jax.nn.sigmoid(b)

    n_chunks = (n_ctx + seq) // c
    n_cc = n_ctx // c

    group = next(g for g in GDN_PREP_GROUPS if n_chunks % g == 0)

    def prep_body(it, carry):
        cis = [it * group + j for j in range(group)]
        r0s = [pl.multiple_of(ci * c, c) for ci in cis]
        keys = [(d, ci, r0) for ci, r0 in zip(cis, r0s) for d in range(2)]
        res = _gdn_prep([q_s[pl.ds(r0, c), :] for _, _, r0 in keys], [k_s[pl.ds(r0, c), :] for _, _, r0 in keys],
                        [v_s[pl.ds(r0, c), :] for _, _, r0 in keys], [gc_s[d, pl.ds(r0, c), :] for d, _, r0 in keys],
                        [bt_s[d, pl.ds(r0, c), :] for d, _, r0 in keys], [d == 1 for d, _, _ in keys])
        for (d, ci, r0), (u, w, qg, attn, kdt, egl) in zip(keys, res):
            u_s[d, pl.ds(r0, c), :] = u
            wq_s[d, ci, 0:c, :] = w
            wq_s[d, ci, c:2 * c, :] = qg
            at_s[d, ci] = attn
            kd_s[d, ci] = kdt
            gl_s[d, ci] = egl
        return carry

    lax.fori_loop(0, n_chunks // group, prep_body, 0)

    def scan_body(t, carry):
        cis = [t, jnp.where(t < n_cc, n_cc - 1 - t, n_chunks - 1 + n_cc - t)]
        r0s = [pl.multiple_of(ci * c, c) for ci in cis]
        wss = [jnp.dot(wq_s[d, cis[d]], carry[d].astype(BF16), preferred_element_type=F32) for d in range(2)]
        v_news = [(u_s[d, pl.ds(r0s[d], c), :] - wss[d][0:c, :]).astype(BF16) for d in range(2)]
        new = [carry[d] * gl_s[d, cis[d]] + jnp.dot(kd_s[d, cis[d]], v_news[d], preferred_element_type=F32)
               for d in range(2)]
        for d in range(2):
            o_s[d, pl.ds(r0s[d], c), :] = (wss[d][c:2 * c, :]
                                           + jnp.dot(at_s[d, cis[d]], v_news[d], preferred_element_type=F32))
        return tuple(new)

    zero = jnp.zeros((HEAD_DIM, HEAD_DIM), F32)
    lax.fori_loop(0, n_chunks, scan_body, (zero, zero))

    def finish(lo, n, z_ref, out_ref):
        o = o_s[0, lo:lo + n, :] + o_s[1, lo:lo + n, :]
        out_ref[...] = (_rms(o) * gn_ref[...] * _silu(z_ref[...])).astype(out_ref.dtype)

    finish(n_ctx, seq, zl_ref, ol_ref)
    if need_ctx:
        finish(0, n_ctx, zc_ref, oc_ref)


def gdn_mixer(p, ab, conv_w, a_log, dt_bias, gdn_g, n_batch, seq, n_ctx, need_ctx):
    cb = (n_batch * seq) // n_ctx
    nh = N_HEADS_HALF
    m = p.shape[0]
    abh = ab[:, 0:4 * nh].reshape(m, 4, nh).transpose(2, 0, 1)
    n_chunks = (seq + n_ctx) // GDN_CHUNK

    def lat(col0):
        return pl.BlockSpec((seq, HEAD_DIM), lambda b, h: (b, col0 + h))

    def ctxs(col0):
        return pl.BlockSpec((n_ctx, HEAD_DIM), lambda b, h: (cb + b, col0 + h))

    in_specs = [lat(0), lat(nh), lat(2 * nh), lat(3 * nh),
                pl.BlockSpec((None, seq, 4), lambda b, h: (h, b, 0)),
                ctxs(0), ctxs(nh), ctxs(2 * nh), ctxs(3 * nh),
                pl.BlockSpec((None, n_ctx, 4), lambda b, h: (h, cb + b, 0)),
                pl.BlockSpec((3, HEAD_DIM), lambda b, h: (0, h)),
                pl.BlockSpec((3, HEAD_DIM), lambda b, h: (0, nh + h)),
                pl.BlockSpec((3, HEAD_DIM), lambda b, h: (0, 2 * nh + h)),
                pl.BlockSpec(memory_space=pltpu.SMEM),
                pl.BlockSpec(memory_space=pltpu.SMEM),
                pl.BlockSpec((1, HEAD_DIM), lambda b, h: (0, 0))]
    out_specs = [pl.BlockSpec((seq, HEAD_DIM), lambda b, h: (b, h))]
    out_shape = [jax.ShapeDtypeStruct((n_batch * seq, HALF_W), BF16)]
    if need_ctx:
        out_specs.append(pl.BlockSpec((n_ctx, HEAD_DIM), lambda b, h: (b, h)))
        out_shape.append(jax.ShapeDtypeStruct((n_batch * n_ctx, HALF_W), BF16))
    tot = seq + n_ctx
    outs = pl.pallas_call(
        functools.partial(_gdn_kernel, seq=seq, n_ctx=n_ctx, need_ctx=need_ctx),
        grid=(n_batch, nh),
        in_specs=in_specs,
        out_specs=out_specs,
        out_shape=out_shape,
        scratch_shapes=[pltpu.VMEM((tot, HEAD_DIM), F32), pltpu.VMEM((tot, HEAD_DIM), F32),
                        pltpu.VMEM((tot, HEAD_DIM), F32),
                        pltpu.VMEM((2, tot, HEAD_DIM), F32), pltpu.VMEM((2, tot, HEAD_DIM), F32),
                        pltpu.VMEM((2, tot, HEAD_DIM), F32), pltpu.VMEM((2, tot, HEAD_DIM), F32),
                        pltpu.VMEM((2, n_chunks, 2 * GDN_CHUNK, HEAD_DIM), BF16),
                        pltpu.VMEM((2, n_chunks, GDN_CHUNK, GDN_CHUNK), BF16),
                        pltpu.VMEM((2, n_chunks, HEAD_DIM, GDN_CHUNK), BF16),
                        pltpu.VMEM((2, n_chunks, 1, HEAD_DIM), F32)],
        compiler_params=_cparams(("arbitrary", "arbitrary")),
        name="gdn",
    )(p, p, p, p, abh, p, p, p, p, abh, conv_w, conv_w, conv_w, a_log, dt_bias, gdn_g.reshape(1, HEAD_DIM))
    return (outs[0], outs[1]) if need_ctx else (outs[0], None)


def _softmax_pv(q, k, v_ext):
    s = _bdot_nt(q, k)
    e = jnp.exp(s - jnp.max(s, axis=-1, keepdims=True)).astype(BF16)
    r = jnp.dot(e, v_ext, preferred_element_type=F32)
    return r[:, 0:HEAD_DIM] / r[:, HEAD_DIM:2 * HEAD_DIM]


def _diff_kernel(*refs, seq, n_ctx, need_ctx, lambda_init):
    (q_ref, kl_ref, vl_ref, kc_ref, vc_ref, cq_ref, sq_ref, ck_ref, sk_ref,
     lq1_ref, lk1_ref, lq2_ref, lk2_ref, sg_ref) = refs[:14]
    if need_ctx:
        qc_ref, o_ref, oc_ref, k1_s, k2_s, v_s = refs[14:]
    else:
        o_ref, k1_s, k2_s, v_s = refs[14:]
    tot = seq + n_ctx
    qscale = DIFF_DH ** -0.5
    lam = (jnp.exp(jnp.sum(lq1_ref[...] * lk1_ref[...], axis=-1, keepdims=True))
           - jnp.exp(jnp.sum(lq2_ref[...] * lk2_ref[...], axis=-1, keepdims=True)) + lambda_init)

    def combine(qv, lo, hi):
        o1 = _softmax_pv(qv[:, 0:DIFF_DH], k1_s[lo:hi, :], v_s[lo:hi, :])
        o2 = _softmax_pv(qv[:, DIFF_DH:2 * DIFF_DH], k2_s[lo:hi, :], v_s[lo:hi, :])
        o = o1 - lam * o2
        return (_rms(o) * sg_ref[...] * (1.0 - lambda_init)).astype(BF16)

    @pl.when(pl.program_id(2) == 0)
    def _():
        k = _rope(kl_ref[...], ck_ref[...], sk_ref[...], DIFF_DH // 4)
        k1_s[0:seq, :] = k[:, 0:DIFF_DH].astype(BF16)
        k2_s[0:seq, :] = k[:, DIFF_DH:2 * DIFF_DH].astype(BF16)
        kc = kc_ref[...]
        k1_s[seq:tot, :] = kc[:, 0:DIFF_DH].astype(BF16)
        k2_s[seq:tot, :] = kc[:, DIFF_DH:2 * DIFF_DH].astype(BF16)
        v_s[0:seq, 0:HEAD_DIM] = vl_ref[...].astype(BF16)
        v_s[seq:tot, 0:HEAD_DIM] = vc_ref[...].astype(BF16)
        v_s[:, HEAD_DIM:2 * HEAD_DIM] = jnp.ones((tot, HEAD_DIM), BF16)
        if need_ctx:
            oc_ref[...] = combine(qc_ref[...] * qscale, seq, tot)

    q = _rope(q_ref[...], cq_ref[...], sq_ref[...], DIFF_DH // 4) * qscale
    o_ref[...] = combine(q, 0, tot)


def diff_mixer(p, cos, sin, lq1, lk1, lq2, lk2, subln_g, lambda_init, n_batch, seq, n_ctx, need_ctx):
    nh = N_HEADS_HALF
    tq = 256
    nq = seq // tq
    cb = (n_batch * seq) // n_ctx
    vec = lambda n: pl.BlockSpec((1, n), lambda b, h, i: (0, 0))
    in_specs = [
        pl.BlockSpec((tq, HEAD_DIM), lambda b, h, i: (b * nq + i, 4 * nh + h)),
        pl.BlockSpec((seq, HEAD_DIM), lambda b, h, i: (b, 5 * nh + h)),
        pl.BlockSpec((seq, HEAD_DIM), lambda b, h, i: (b, 6 * nh + h)),
        pl.BlockSpec((n_ctx, HEAD_DIM), lambda b, h, i: (cb + b, 5 * nh + h)),
        pl.BlockSpec((n_ctx, HEAD_DIM), lambda b, h, i: (cb + b, 6 * nh + h)),
        pl.BlockSpec((tq, HEAD_DIM), lambda b, h, i: (i, 0)),
        pl.BlockSpec((tq, HEAD_DIM), lambda b, h, i: (i, 0)),
        pl.BlockSpec((seq, HEAD_DIM), lambda b, h, i: (0, 0)),
        pl.BlockSpec((seq, HEAD_DIM), lambda b, h, i: (0, 0)),
        vec(DIFF_DH), vec(DIFF_DH), vec(DIFF_DH), vec(DIFF_DH), vec(HEAD_DIM),
    ]
    args = [p, p, p, p, p, cos, sin, cos, sin,
            lq1.reshape(1, -1), lk1.reshape(1, -1), lq2.reshape(1, -1), lk2.reshape(1, -1), subln_g.reshape(1, -1)]
    out_specs = [pl.BlockSpec((tq, HEAD_DIM), lambda b, h, i: (b * nq + i, h))]
    out_shape = [jax.ShapeDtypeStruct((n_batch * seq, HALF_W), BF16)]
    if need_ctx:
        in_specs.append(pl.BlockSpec((n_ctx, HEAD_DIM), lambda b, h, i: (cb + b, 4 * nh + h)))
        args.append(p)
        out_specs.append(pl.BlockSpec((n_ctx, HEAD_DIM), lambda b, h, i: (b, h)))
        out_shape.append(jax.ShapeDtypeStruct((n_batch * n_ctx, HALF_W), BF16))
    tot = seq + n_ctx
    outs = pl.pallas_call(
        functools.partial(_diff_kernel, seq=seq, n_ctx=n_ctx, need_ctx=need_ctx, lambda_init=lambda_init),
        grid=(n_batch, nh, nq),
        in_specs=in_specs,
        out_specs=out_specs,
        out_shape=out_shape,
        scratch_shapes=[pltpu.VMEM((tot, DIFF_DH), BF16), pltpu.VMEM((tot, DIFF_DH), BF16),
                        pltpu.VMEM((tot, 2 * HEAD_DIM), BF16)],
        compiler_params=_cparams(("arbitrary", "arbitrary", "arbitrary")),
        name="diff_attn",
    )(*args)
    return (outs[0], outs[1]) if need_ctx else (outs[0], None)


def _na_bias_table(rpb, n_rows):
    kr = min(NA_ROWS, n_rows)
    c = np.arange(GRID_W)[:, None]
    kc = np.arange(GRID_W)[None, :]
    win_c = np.clip(c - NA_COLS // 2, 0, GRID_W - NA_COLS)
    ok = (kc >= win_c) & (kc < win_c + NA_COLS)
    dc = np.clip(kc - c + NA_COLS - 1, 0, 2 * NA_COLS - 2)
    onehot = (dc[None] == np.arange(2 * NA_COLS - 1)[:, None, None]).astype(np.float32)
    e = jnp.einsum('hdr,rck->hdck', rpb, jnp.asarray(onehot), precision=lax.Precision.HIGHEST)
    e = jnp.where(jnp.asarray(ok)[None, None], e, NEG_BIG)
    tab = jnp.stack([e[:, NA_ROWS - 1 - o:NA_ROWS - 1 - o + kr] for o in range(NA_ROWS)], axis=1)
    return tab.transpose(0, 1, 3, 2, 4).reshape(rpb.shape[0], NA_ROWS, GRID_W, kr * GRID_W)


def _na_kernel(*refs, seq, n_ctx, need_ctx):
    q_ref, k_ref, v_ref, kc_ref, vc_ref, bias_ref = refs[:6]
    if need_ctx:
        qc_ref, o_ref, oc_ref, k_s, v_s, kc_s, vc_s = refs[6:]
    else:
        o_ref, k_s, v_s, kc_s, vc_s = refs[6:]
    n_rows = seq // GRID_W
    kr = min(NA_ROWS, n_rows)
    nl = kr * GRID_W
    scale = HEAD_DIM ** -0.5
    k_s[...] = k_ref[...].astype(BF16)
    v_s[...] = v_ref[...].astype(BF16)
    kc_s[...] = kc_ref[...].astype(BF16)
    vc_s[...] = vc_ref[...].astype(BF16)

    def row_group(it, carry):
        rs = [it * NA_ROW_GROUP + j for j in range(NA_ROW_GROUP)]
        q0s = [pl.multiple_of(r * GRID_W, GRID_W) for r in rs]
        starts = [jnp.clip(r - kr // 2, 0, n_rows - kr) for r in rs]
        k0s = [pl.multiple_of(st * GRID_W, GRID_W) for st in starts]
        qs = [q_ref[pl.ds(q0, GRID_W), :] for q0 in q0s]
        s_locs = _each(lambda q, k0, r, st: _bdot_nt(q, k_s[pl.ds(k0, nl), :]) * scale + bias_ref[r - st],
                       qs, k0s, rs, starts)
        s_ctxs = [_bdot_nt(q, kc_s[...]) * scale for q in qs]
        ms = _each(lambda a, b: jnp.maximum(jnp.max(a, axis=-1, keepdims=True), jnp.max(b, axis=-1, keepdims=True)),
                   s_locs, s_ctxs)
        e_locs = _each(lambda s, m: jnp.exp(s - m), s_locs, ms)
        e_ctxs = _each(lambda s, m: jnp.exp(s - m), s_ctxs, ms)
        dens = _each(lambda a, b: jnp.sum(a, axis=-1, keepdims=True) + jnp.sum(b, axis=-1, keepdims=True),
                     e_locs, e_ctxs)
        os_ = _each(lambda el, ec, k0: _bdot(el, v_s[pl.ds(k0, nl), :]) + _bdot(ec, vc_s[...]), e_locs, e_ctxs, k0s)
        for q0, o, den in zip(q0s, os_, dens):
            o_ref[pl.ds(q0, GRID_W), :] = (o / den).astype(BF16)
        return carry

    lax.fori_loop(0, n_rows // NA_ROW_GROUP, row_group, 0)
    if need_ctx:
        s = _bdot_nt(qc_ref[...], kc_s[...]) * scale
        e = jnp.exp(s - jnp.max(s, axis=-1, keepdims=True))
        oc_ref[...] = (_bdot(e, vc_s[...]) / jnp.sum(e, axis=-1, keepdims=True)).astype(BF16)


def na_mixer(p, bias_tab, n_batch, seq, n_ctx, need_ctx):
    nh = N_HEADS_HALF
    cb = (n_batch * seq) // n_ctx
    nl = bias_tab.shape[-1]
    in_specs = [
        pl.BlockSpec((seq, HEAD_DIM), lambda b, h: (b, h)),
        pl.BlockSpec((seq, HEAD_DIM), lambda b, h: (b, nh + h)),
        pl.BlockSpec((seq, HEAD_DIM), lambda b, h: (b, 2 * nh + h)),
        pl.BlockSpec((n_ctx, HEAD_DIM), lambda b, h: (cb + b, nh + h)),
        pl.BlockSpec((n_ctx, HEAD_DIM), lambda b, h: (cb + b, 2 * nh + h)),
        pl.BlockSpec((None, NA_ROWS, GRID_W, nl), lambda b, h: (h, 0, 0, 0)),
    ]
    args = [p, p, p, p, p, bias_tab]
    out_specs = [pl.BlockSpec((seq, HEAD_DIM), lambda b, h: (b, h))]
    out_shape = [jax.ShapeDtypeStruct((n_batch * seq, HALF_W), BF16)]
    if need_ctx:
        in_specs.append(pl.BlockSpec((n_ctx, HEAD_DIM), lambda b, h: (cb + b, h)))
        args.append(p)
        out_specs.append(pl.BlockSpec((n_ctx, HEAD_DIM), lambda b, h: (b, h)))
        out_shape.append(jax.ShapeDtypeStruct((n_batch * n_ctx, HALF_W), BF16))
    outs = pl.pallas_call(
        functools.partial(_na_kernel, seq=seq, n_ctx=n_ctx, need_ctx=need_ctx),
        grid=(n_batch, nh),
        in_specs=in_specs,
        out_specs=out_specs,
        out_shape=out_shape,
        scratch_shapes=[pltpu.VMEM((seq, HEAD_DIM), BF16), pltpu.VMEM((seq, HEAD_DIM), BF16),
                        pltpu.VMEM((n_ctx, HEAD_DIM), BF16), pltpu.VMEM((n_ctx, HEAD_DIM), BF16)],
        compiler_params=_cparams(("arbitrary", "arbitrary")),
        name="na_attn",
    )(*args)
    return (outs[0], outs[1]) if need_ctx else (outs[0], None)


def _win_kernel(*refs, seq, n_ctx, need_ctx):
    q_ref, k_ref, v_ref, kc_ref, vc_ref, cos_ref, sin_ref, sink_ref = refs[:8]
    if need_ctx:
        qc_ref, o_ref, oc_ref, k_s, v_s, kc_s, vc_s = refs[8:]
    else:
        o_ref, k_s, v_s, kc_s, vc_s = refs[8:]
    kvh = pl.program_id(1)
    wb = WIN_BLOCK
    band = 3 * wb
    scale = HEAD_DIM ** -0.5
    k_s[...] = _rope(k_ref[...], cos_ref[...], sin_ref[...], HEAD_DIM // 4).astype(BF16)
    v_s[...] = v_ref[...].astype(BF16)
    kc_s[...] = kc_ref[...].astype(BF16)
    vc_s[...] = vc_ref[...].astype(BF16)
    sinks = [jnp.full((1, 1), sink_ref[kvh * WIN_GROUP + j], F32) for j in range(WIN_GROUP)]

    def attend(q, j, k_loc, v_loc, valid):
        s_ctx = _bdot_nt(q, kc_s[...]) * scale
        m = jnp.maximum(jnp.max(s_ctx, axis=-1, keepdims=True), sinks[j])
        if k_loc is not None:
            s_loc = jnp.where(valid, _bdot_nt(q, k_loc) * scale, NEG_BIG)
            m = jnp.maximum(m, jnp.max(s_loc, axis=-1, keepdims=True))
        e_ctx = jnp.exp(s_ctx - m)
        den = jnp.sum(e_ctx, axis=-1, keepdims=True) + jnp.exp(sinks[j] - m)
        o = _bdot(e_ctx, vc_s[...])
        if k_loc is not None:
            e_loc = jnp.exp(s_loc - m)
            den = den + jnp.sum(e_loc, axis=-1, keepdims=True)
            o = o + _bdot(e_loc, v_loc)
        return (o / den).astype(BF16)

    def q_block(n, carry):
        q0 = pl.multiple_of(n * wb, wb)
        k0 = pl.multiple_of(jnp.clip(n * wb - wb, 0, seq - band), wb)
        qpos = q0 + lax.broadcasted_iota(jnp.int32, (wb, band), 0)
        kpos = k0 + lax.broadcasted_iota(jnp.int32, (wb, band), 1)
        valid = jnp.abs(qpos - kpos) <= WIN
        k_loc = k_s[pl.ds(k0, band), :]
        v_loc = v_s[pl.ds(k0, band), :]
        cos = cos_ref[pl.ds(q0, wb), :]
        sin = sin_ref[pl.ds(q0, wb), :]
        for j in range(WIN_GROUP):
            q = _rope(q_ref[pl.ds(q0, wb), j * HEAD_DIM:(j + 1) * HEAD_DIM], cos, sin, HEAD_DIM // 4)
            o_ref[pl.ds(q0, wb), j * HEAD_DIM:(j + 1) * HEAD_DIM] = attend(q, j, k_loc, v_loc, valid)
        return carry

    lax.fori_loop(0, seq // wb, q_block, 0)
    if need_ctx:
        for j in range(WIN_GROUP):
            q = qc_ref[:, j * HEAD_DIM:(j + 1) * HEAD_DIM]
            oc_ref[:, j * HEAD_DIM:(j + 1) * HEAD_DIM] = attend(q, j, None, None, None)


def win_mixer(p, cos, sin, sink, n_batch, seq, n_ctx, need_ctx):
    nh = N_HEADS_HALF
    cb = (n_batch * seq) // n_ctx
    gw = WIN_GROUP * HEAD_DIM
    q_blk0 = (3 * nh * HEAD_DIM) // gw
    in_specs = [
        pl.BlockSpec((seq, gw), lambda b, g: (b, q_blk0 + g)),
        pl.BlockSpec((seq, HEAD_DIM), lambda b, g: (b, 4 * nh + g)),
        pl.BlockSpec((seq, HEAD_DIM), lambda b, g: (b, 4 * nh + WIN_KV_HEADS + g)),
        pl.BlockSpec((n_ctx, HEAD_DIM), lambda b, g: (cb + b, 4 * nh + g)),
        pl.BlockSpec((n_ctx, HEAD_DIM), lambda b, g: (cb + b, 4 * nh + WIN_KV_HEADS + g)),
        pl.BlockSpec((seq, HEAD_DIM), lambda b, g: (0, 0)),
        pl.BlockSpec((seq, HEAD_DIM), lambda b, g: (0, 0)),
        pl.BlockSpec(memory_space=pltpu.SMEM),
    ]
    args = [p, p, p, p, p, cos, sin, sink]
    out_specs = [pl.BlockSpec((seq, gw), lambda b, g: (b, g))]
    out_shape = [jax.ShapeDtypeStruct((n_batch * seq, HALF_W), BF16)]
    if need_ctx:
        in_specs.append(pl.BlockSpec((n_ctx, gw), lambda b, g: (cb + b, q_blk0 + g)))
        args.append(p)
        out_specs.append(pl.BlockSpec((n_ctx, gw), lambda b, g: (b, g)))
        out_shape.append(jax.ShapeDtypeStruct((n_batch * n_ctx, HALF_W), BF16))
    outs = pl.pallas_call(
        functools.partial(_win_kernel, seq=seq, n_ctx=n_ctx, need_ctx=need_ctx),
        grid=(n_batch, WIN_KV_HEADS),
        in_specs=in_specs,
        out_specs=out_specs,
        out_shape=out_shape,
        scratch_shapes=[pltpu.VMEM((seq, HEAD_DIM), BF16), pltpu.VMEM((seq, HEAD_DIM), BF16),
                        pltpu.VMEM((n_ctx, HEAD_DIM), BF16), pltpu.VMEM((n_ctx, HEAD_DIM), BF16)],
        compiler_params=_cparams(("arbitrary", "arbitrary")),
        name="win_attn",
    )(*args)
    return (outs[0], outs[1]) if need_ctx else (outs[0], None)


def kernel(x, c, ctx, c_ctx, ada_w, ada_b, norm_mix_g, norm_ffn_g, w_in_even, gdn_conv_w, gdn_a_log, gdn_dt_bias,
           gdn_norm_g, diff_lambda_q1, diff_lambda_k1, diff_lambda_q2, diff_lambda_k2, diff_subln_g, w_in_odd,
           na_rpb, win_sink, w_out, ffn_w_gate, ffn_w_up, ffn_w_down, final_norm_g):
    n_batch, seq, d = x.shape
    n_ctx = ctx.shape[1]
    n_lat = n_batch * seq
    assert n_batch + 1 <= MOD_ROWS and seq % GRID_W == 0
    hf = jnp.concatenate([x.reshape(n_lat, d), ctx.reshape(n_batch * n_ctx, d)], axis=0)
    cc = jnp.zeros((MOD_ROWS, d), F32).at[:n_batch].set(c).at[n_batch].set(c_ctx)
    mods = ada_mod(cc, ada_w, ada_b)

    cos64, sin64 = _rope_tables(seq, DIFF_DH)
    cos128, sin128 = _rope_tables(seq, HEAD_DIM)

    qkvz = 4 * HALF_W
    for l in range(DEPTH):
        need_ctx = l < DEPTH - 1
        i = l // 2
        if l % 2 == 0:
            w = w_in_even[i]
            w_main = jnp.concatenate([w[:, :qkvz], w[:, qkvz + 4 * N_HEADS_HALF:]], axis=1).astype(BF16)
            w_ab = jnp.pad(w[:, qkvz:qkvz + 4 * N_HEADS_HALF], ((0, 0), (0, HEAD_DIM - 4 * N_HEADS_HALF))).astype(BF16)
            p, ab = in_proj(hf, mods[l], norm_mix_g[l], w_main, w_ab, n_lat, seq, 1024)
            lambda_init = 0.8 - 0.6 * float(np.exp(-0.3 * l))
            ya = gdn_mixer(p, ab, gdn_conv_w[i], gdn_a_log[i], gdn_dt_bias[i], gdn_norm_g[i],
                           n_batch, seq, n_ctx, need_ctx)
            yb = diff_mixer(p, cos64, sin64, diff_lambda_q1[i], diff_lambda_k1[i], diff_lambda_q2[i],
                            diff_lambda_k2[i], diff_subln_g[i], lambda_init, n_batch, seq, n_ctx, need_ctx)
        else:
            p = in_proj(hf, mods[l], norm_mix_g[l], w_in_odd[i].astype(BF16), None, n_lat, seq, 1536)
            bias_tab = _na_bias_table(na_rpb[i], seq // GRID_W)
            ya = na_mixer(p, bias_tab, n_batch, seq, n_ctx, need_ctx)
            yb = win_mixer(p, cos128, sin128, win_sink[i], n_batch, seq, n_ctx, need_ctx)
        y_ctx = (ya[1], yb[1]) if need_ctx else None
        h1, u2 = out_proj((ya[0], yb[0]), y_ctx, hf, mods[l], norm_ffn_g[l], w_out[l].astype(BF16), n_lat, seq)
        hf = ffn(u2, h1, mods[l], ffn_w_gate[l].astype(BF16), ffn_w_up[l].astype(BF16), ffn_w_down[l].astype(BF16),
                 final_norm_g, n_lat, seq, final=not need_ctx)
    return hf[:n_lat].reshape(n_batch, seq, d)
```

```python
import functools

import jax
import jax.numpy as jnp
import numpy as np
from jax import lax
from jax.experimental import pallas as pl
from jax.experimental.pallas import tpu as pltpu

F32 = jnp.float32
BF16 = jnp.bfloat16

DEPTH = 4
GRID_W = 64
HEAD_DIM = 128
N_HEADS_HALF = 8
HALF_W = N_HEADS_HALF * HEAD_DIM
GDN_CHUNK = 64
DIFF_DH = 64
NA_ROWS = 8
NA_COLS = 16
NA_ROW_GROUP = 4
WIN = 128
WIN_BLOCK = 128
WIN_KV_HEADS = 2
WIN_GROUP = N_HEADS_HALF // WIN_KV_HEADS
ROPE_BASE = 10000.0
NORM_EPS = 1e-6
LOG2E = 1.4426950408889634
NEG_BIG = -1e30

V7X_VMEM_LIMIT = 56 * 1024 * 1024
MOD_ROWS = 16


def _cparams(sem):
    return pltpu.CompilerParams(dimension_semantics=sem, vmem_limit_bytes=V7X_VMEM_LIMIT)


def _bdot(a, b):
    return jnp.dot(a.astype(BF16), b.astype(BF16), preferred_element_type=F32)


def _bdot_nt(a, b):
    return lax.dot_general(a.astype(BF16), b.astype(BF16), (((1,), (1,)), ((), ())),
                           preferred_element_type=F32)


def _bdot_tn(a, b):
    return lax.dot_general(a.astype(BF16), b.astype(BF16), (((0,), (0,)), ((), ())),
                           preferred_element_type=F32)


def _hdot(a, b):
    return jnp.dot(a, b, preferred_element_type=F32, precision=lax.Precision.HIGHEST)


def _silu(x):
    return x * jax.nn.sigmoid(x)


def _rms(x):
    return x * lax.rsqrt(jnp.mean(x * x, axis=-1, keepdims=True) + NORM_EPS)


def _row_tile(n_lat, n_ctx, cap):
    for t in (1024, 512, 256):
        if t <= cap and n_lat % t == 0 and n_ctx % t == 0:
            return t
    raise ValueError("row counts must be multiples of 256")


def _ada_kernel(c_ref, w_ref, b_ref, o_ref):
    o_ref[...] = _bdot(_silu(c_ref[...]), w_ref[...]) + b_ref[...]


def ada_mod(cc, ada_w, ada_b):
    depth, d, n6 = ada_w.shape
    tn = 1024
    out = pl.pallas_call(
        _ada_kernel,
        grid=(depth, n6 // tn),
        in_specs=[
            pl.BlockSpec((MOD_ROWS, d), lambda l, j: (0, 0)),
            pl.BlockSpec((None, d, tn), lambda l, j: (l, 0, j)),
            pl.BlockSpec((None, 1, tn), lambda l, j: (l, 0, j)),
        ],
        out_specs=pl.BlockSpec((None, MOD_ROWS, tn), lambda l, j: (l, 0, j)),
        out_shape=jax.ShapeDtypeStruct((depth, MOD_ROWS, n6), F32),
        compiler_params=_cparams(("arbitrary", "arbitrary")),
        name="ada_mod",
    )(cc, ada_w, ada_b.reshape(depth, 1, n6))
    return out.reshape(depth, MOD_ROWS, 6, d)


def _norm_mod(x, g, shift, scale):
    return (_rms(x) * g) * (1.0 + scale) + shift


def _seg_fn(n_lat, seq, tm):
    n_lat_tiles = n_lat // tm
    n_batch = n_lat // seq
    return lambda i: jnp.where(i < n_lat_tiles, (i * tm) // seq, n_batch)


def _inproj_kernel(h_ref, mod_ref, g_ref, w_ref, *rest, has_aux):
    if has_aux:
        waux_ref, o_ref, oaux_ref, u_scr = rest
    else:
        o_ref, u_scr = rest

    @pl.when(pl.program_id(1) == 0)
    def _():
        u = _norm_mod(h_ref[...], g_ref[...], mod_ref[0:1, :], mod_ref[1:2, :])
        u_scr[...] = u.astype(BF16)
        if has_aux:
            oaux_ref[...] = jnp.dot(u_scr[...], waux_ref[...], preferred_element_type=F32)

    o_ref[...] = jnp.dot(u_scr[...], w_ref[...], preferred_element_type=F32)


def in_proj(hf, mod_l, g, w, w_aux, n_lat, seq, tn):
    m, d = hf.shape
    n = w.shape[1]
    tm = _row_tile(n_lat, m - n_lat, 1024)
    seg = _seg_fn(n_lat, seq, tm)
    has_aux = w_aux is not None
    in_specs = [
        pl.BlockSpec((tm, d), lambda i, j: (i, 0)),
        pl.BlockSpec((None, 6, d), lambda i, j: (seg(i), 0, 0)),
        pl.BlockSpec((1, d), lambda i, j: (0, 0)),
        pl.BlockSpec((d, tn), lambda i, j: (0, j)),
    ]
    out_specs = [pl.BlockSpec((tm, tn), lambda i, j: (i, j))]
    out_shape = [jax.ShapeDtypeStruct((m, n), F32)]
    args = [hf, mod_l, g.reshape(1, d), w]
    if has_aux:
        in_specs.append(pl.BlockSpec((d, HEAD_DIM), lambda i, j: (0, 0)))
        out_specs.append(pl.BlockSpec((tm, HEAD_DIM), lambda i, j: (i, 0)))
        out_shape.append(jax.ShapeDtypeStruct((m, HEAD_DIM), F32))
        args.append(w_aux)
    outs = pl.pallas_call(
        functools.partial(_inproj_kernel, has_aux=has_aux),
        grid=(m // tm, n // tn),
        in_specs=in_specs,
        out_specs=out_specs,
        out_shape=out_shape,
        scratch_shapes=[pltpu.VMEM((tm, d), BF16)],
        compiler_params=_cparams(("arbitrary", "arbitrary")),
        name="in_proj",
    )(*args)
    return outs if has_aux else outs[0]


def _outproj_kernel(*refs, n_lat_tiles, has_ctx):
    if has_ctx:
        yal_ref, ybl_ref, yac_ref, ybc_ref, h_ref, mod_ref, g_ref, w_ref, ho_ref, u_ref = refs
        is_ctx = pl.program_id(0) >= n_lat_tiles
        ya = jnp.where(is_ctx, yac_ref[...], yal_ref[...])
        yb = jnp.where(is_ctx, ybc_ref[...], ybl_ref[...])
    else:
        yal_ref, ybl_ref, h_ref, mod_ref, g_ref, w_ref, ho_ref, u_ref = refs
        ya, yb = yal_ref[...], ybl_ref[...]
    y = (jnp.dot(ya, w_ref[0:HALF_W, :], preferred_element_type=F32)
         + jnp.dot(yb, w_ref[HALF_W:2 * HALF_W, :], preferred_element_type=F32))
    hn = h_ref[...] + mod_ref[2:3, :] * y
    ho_ref[...] = hn
    u_ref[...] = _norm_mod(hn, g_ref[...], mod_ref[3:4, :], mod_ref[4:5, :]).astype(BF16)


def out_proj(y_lat, y_ctx, hf, mod_l, g, w, n_lat, seq):
    m, d = hf.shape
    has_ctx = y_ctx is not None
    m_out = m if has_ctx else n_lat
    tm = _row_tile(n_lat, m - n_lat, 512)
    n_lat_tiles = n_lat // tm
    seg = _seg_fn(n_lat, seq, tm)
    lat_spec = pl.BlockSpec((tm, HALF_W), lambda i: (jnp.minimum(i, n_lat_tiles - 1), 0))
    ctx_spec = pl.BlockSpec((tm, HALF_W), lambda i: (jnp.maximum(i - n_lat_tiles, 0), 0))
    in_specs = [lat_spec, lat_spec] + ([ctx_spec, ctx_spec] if has_ctx else []) + [
        pl.BlockSpec((tm, d), lambda i: (i, 0)),
        pl.BlockSpec((None, 6, d), lambda i: (seg(i), 0, 0)),
        pl.BlockSpec((1, d), lambda i: (0, 0)),
        pl.BlockSpec((d, d), lambda i: (0, 0)),
    ]
    args = list(y_lat) + (list(y_ctx) if has_ctx else []) + [hf, mod_l, g.reshape(1, d), w]
    return pl.pallas_call(
        functools.partial(_outproj_kernel, n_lat_tiles=n_lat_tiles, has_ctx=has_ctx),
        grid=(m_out // tm,),
        in_specs=in_specs,
        out_specs=[pl.BlockSpec((tm, d), lambda i: (i, 0)), pl.BlockSpec((tm, d), lambda i: (i, 0))],
        out_shape=[jax.ShapeDtypeStruct((m_out, d), F32), jax.ShapeDtypeStruct((m_out, d), BF16)],
        compiler_params=_cparams(("arbitrary",)),
        name="out_proj",
    )(*args)


def _ffn_kernel(u_ref, h_ref, mod_ref, wg_ref, wu_ref, wd_ref, fg_ref, o_ref, acc_ref, *, final):
    j = pl.program_id(1)

    @pl.when(j == 0)
    def _():
        acc_ref[...] = jnp.zeros_like(acc_ref)

    u = u_ref[...]
    gate = jnp.dot(u, wg_ref[...], preferred_element_type=F32)
    up = jnp.dot(u, wu_ref[...], preferred_element_type=F32)
    act = (_silu(gate) * up).astype(BF16)
    acc_ref[...] = acc_ref[...] + jnp.dot(act, wd_ref[...], preferred_element_type=F32)

    @pl.when(j == pl.num_programs(1) - 1)
    def _():
        hn = h_ref[...] + mod_ref[5:6, :] * acc_ref[...]
        if final:
            hn = _rms(hn) * fg_ref[...]
        o_ref[...] = hn


def ffn(u2, h1, mod_l, wg, wu, wd, final_g, n_lat, seq, final):
    m, d = h1.shape
    hidden = wg.shape[1]
    tm = _row_tile(n_lat, m - n_lat, 512)
    th = 512
    seg = _seg_fn(n_lat, seq, tm)
    return pl.pallas_call(
        functools.partial(_ffn_kernel, final=final),
        grid=(m // tm, hidden // th),
        in_specs=[
            pl.BlockSpec((tm, d), lambda i, j: (i, 0)),
            pl.BlockSpec((tm, d), lambda i, j: (i, 0)),
            pl.BlockSpec((None, 6, d), lambda i, j: (seg(i), 0, 0)),
            pl.BlockSpec((d, th), lambda i, j: (0, j)),
            pl.BlockSpec((d, th), lambda i, j: (0, j)),
            pl.BlockSpec((th, d), lambda i, j: (j, 0)),
            pl.BlockSpec((1, d), lambda i, j: (0, 0)),
        ],
        out_specs=pl.BlockSpec((tm, d), lambda i, j: (i, 0)),
        out_shape=jax.ShapeDtypeStruct((m, d), F32),
        scratch_shapes=[pltpu.VMEM((tm, d), F32)],
        compiler_params=_cparams(("arbitrary", "arbitrary")),
        name="ffn",
    )(u2, h1, mod_l, wg, wu, wd, final_g.reshape(1, d))


def _rope_tables(seq, comp):
    pos = np.arange(seq)
    rows, cols = pos // GRID_W, pos % GRID_W
    n = comp // 2
    inv = (ROPE_BASE ** (-np.arange(0, n, 2, dtype=np.float32) / n)).astype(np.float32)
    ang_r = rows.astype(np.float32)[:, None] * inv[None, :]
    ang_c = cols.astype(np.float32)[:, None] * inv[None, :]
    ang = jnp.asarray(np.concatenate([ang_r, ang_r, ang_c, ang_c], axis=1))
    sign = np.concatenate([-np.ones(n // 2), np.ones(n // 2)] * 2).astype(np.float32)
    reps = HEAD_DIM // comp
    return jnp.tile(jnp.cos(ang), (1, reps)), jnp.tile(jnp.sin(ang) * jnp.asarray(sign)[None, :], (1, reps))


def _rope(x, cos, sin, half):
    lane = lax.broadcasted_iota(jnp.int32, x.shape, 1)
    first = (lane % (2 * half)) < half
    fwd = pltpu.roll(x, HEAD_DIM - half, 1)
    bwd = pltpu.roll(x, half, 1)
    return x * cos + jnp.where(first, fwd, bwd) * sin


def _shift_rows(x, up):
    t = x.shape[0]
    row = lax.broadcasted_iota(jnp.int32, x.shape, 0)
    if up:
        return jnp.where(row == t - 1, 0.0, pltpu.roll(x, t - 1, 0))
    return jnp.where(row == 0, 0.0, pltpu.roll(x, 1, 0))


def _conv_silu(x, w):
    y = _shift_rows(x, False) * w[0:1, :] + x * w[1:2, :] + _shift_rows(x, True) * w[2:3, :]
    return _silu(y)


def _l2n(x):
    return x * lax.rsqrt(jnp.sum(x * x, axis=-1, keepdims=True) + NORM_EPS)


def _softplus(x):
    return jnp.maximum(x, 0.0) + jnp.log1p(jnp.exp(-jnp.abs(x)))


TRI_BASE = 8
GDN_PREP_GROUPS = (6, 4, 2, 1)


def _each(fn, *lists):
    return [fn(*xs) for xs in zip(*lists)]


def _weave(*gens):
    results = [None] * len(gens)
    live = list(range(len(gens)))
    while live:
        for i in list(live):
            try:
                next(gens[i])
            except StopIteration as stop:
                results[i] = stop.value
                live.remove(i)
    return results


def _unit_tri_inverse_stages(mats):
    c = mats[0].shape[0]
    row = lax.broadcasted_iota(jnp.int32, (c, c), 0)
    col = lax.broadcasted_iota(jnp.int32, (c, c), 1)
    same = lambda n: (row // n) == (col // n)
    eye = (row == col).astype(F32)
    xs = [jnp.where(same(TRI_BASE), -a, 0.0) for a in mats]
    ts = [eye + x for x in xs]
    p = 2
    while p < TRI_BASE:
        xs = [_bdot(x, x) for x in xs]
        yield
        ts = _each(lambda t, x: t + _bdot(t, x), ts, xs)
        yield
        p *= 2
    n = 2 * TRI_BASE
    while n <= c:
        off = same(n) & jnp.logical_not(same(n // 2))
        tas = _each(lambda t, a: _bdot(t, jnp.where(off, a, 0.0)), ts, mats)
        yield
        ts = _each(lambda t, ta: t - _bdot(ta, t), ts, tas)
        yield
        n *= 2
    return ts


def _seg_cumsum(x, rev):
    t = x.shape[0]
    pos = lax.broadcasted_iota(jnp.int32, x.shape, 0) % GDN_CHUNK
    s = 1
    while s < GDN_CHUNK:
        if rev:
            x = x + jnp.where(pos < GDN_CHUNK - s, pltpu.roll(x, t - s, 0), 0.0)
        else:
            x = x + jnp.where(pos >= s, pltpu.roll(x, s, 0), 0.0)
        s *= 2
    return x


def _gdn_prep_stages(qs, ks, vs, gcs, betas, revs):
    c = qs[0].shape[0]
    row = lax.broadcasted_iota(jnp.int32, (c, c), 0)
    col = lax.broadcasted_iota(jnp.int32, (c, c), 1)
    tri = {False: row >= col, True: row <= col}
    strict = {False: row > col, True: row < col}
    decays = _each(lambda gc, rev: jnp.exp(jnp.where(tri[rev], gc[:, 0:c] - jnp.transpose(gc)[0:c, :], -jnp.inf)),
                   gcs, revs)
    kbs = _each(lambda k, b: k * b, ks, betas)
    kks = _each(_bdot_nt, kbs, ks)
    qsc = [q * (HEAD_DIM ** -0.5) for q in qs]
    qks = _each(_bdot_nt, qsc, ks)
    yield
    a_mats = _each(lambda kk, dec, rev: jnp.where(strict[rev], kk * dec, 0.0), kks, decays, revs)
    t_mats = yield from _unit_tri_inverse_stages(a_mats)
    egs = [jnp.exp(gc) for gc in gcs]
    rhs = _each(lambda v, b, kb, eg: jnp.concatenate([v * b, kb * eg], axis=1), vs, betas, kbs, egs)
    uws = _each(_bdot, t_mats, rhs)
    yield
    attns = _each(lambda qk, dec, rev: jnp.where(tri[rev], qk * dec, 0.0).astype(BF16), qks, decays, revs)
    g_lasts = _each(lambda gc, rev: gc[0:1, :] if rev else gc[c - 1:c, :], gcs, revs)
    kdts = _each(lambda k, gl, gc: jnp.transpose(k * jnp.exp(gl - gc)).astype(BF16), ks, g_lasts, gcs)
    return [(uw[:, 0:HEAD_DIM], uw[:, HEAD_DIM:].astype(BF16), (q * eg).astype(BF16), attn, kdt, jnp.exp(gl))
            for uw, q, eg, attn, kdt, gl in zip(uws, qsc, egs, attns, kdts, g_lasts)]


def _gdn_kernel(*refs, seq, n_ctx, need_ctx):
    (ql_ref, kl_ref, vl_ref, zl_ref, abl_ref, qc_ref, kc_ref, vc_ref, zc_ref, abc_ref,
     cwq_ref, cwk_ref, cwv_ref, alog_ref, dtb_ref, gn_ref) = refs[:16]
    if need_ctx:
        ol_ref, oc_ref = refs[16:18]
        scr = refs[18:]
    else:
        ol_ref = refs[16]
        scr = refs[17:]
    q_s, k_s, v_s, gc_s, bt_s, u_s, o_s, wq_s, at_s, kd_s, gl_s = scr
    hd = pl.program_id(1)
    c = GDN_CHUNK
    for (src_q, src_k, src_v, lo, n) in ((qc_ref, kc_ref, vc_ref, 0, n_ctx), (ql_ref, kl_ref, vl_ref, n_ctx, seq)):
        q_s[lo:lo + n, :] = _l2n(_conv_silu(src_q[...], cwq_ref[...]))
        k_s[lo:lo + n, :] = _l2n(_conv_silu(src_k[...], cwk_ref[...]))
        v_s[lo:lo + n, :] = _conv_silu(src_v[...], cwv_ref[...])
    lane4 = lax.broadcasted_iota(jnp.int32, (1, 4), 1)
    neg_decay = -jnp.exp(jnp.where(lane4 == 0, alog_ref[0, hd], alog_ref[1, hd]))
    dt_b = jnp.where(lane4 == 0, dtb_ref[0, hd], dtb_ref[1, hd])
    for (src, lo, n) in ((abc_ref, 0, n_ctx), (abl_ref, n_ctx, seq)):
        ab = src[...]
        g = neg_decay * _softplus(ab + dt_b)
        beta = jax.nn.sigmoid(ab)
        for d in range(2):
            gc_s[d, lo:lo + n, :] = jnp.broadcast_to(_seg_cumsum(g, d == 1)[:, d:d + 1], (n, HEAD_DIM))
            bt_s[d, lo:lo + n, :] = jnp.broadcast_to(beta[:, 2 + d:3 + d], (n, HEAD_DIM))

    n_chunks = (n_ctx + seq) // c
    n_cc = n_ctx // c

    group = next(g for g in GDN_PREP_GROUPS if n_chunks % g == 0)
    n_groups = n_chunks // group

    def chunk_of(d, t):
        return t if d == 0 else jnp.where(t < n_cc, n_cc - 1 - t, n_chunks - 1 + n_cc - t)

    def prep_group(g):
        keys = [(d, g * group + j) for j in range(group) for d in range(2)]
        r0s = [pl.multiple_of(chunk_of(d, t) * c, c) for d, t in keys]
        res = yield from _gdn_prep_stages(
            [q_s[pl.ds(r0, c), :] for r0 in r0s], [k_s[pl.ds(r0, c), :] for r0 in r0s],
            [v_s[pl.ds(r0, c), :] for r0 in r0s], [gc_s[d, pl.ds(r0, c), :] for (d, _), r0 in zip(keys, r0s)],
            [bt_s[d, pl.ds(r0, c), :] for (d, _), r0 in zip(keys, r0s)], [d == 1 for d, _ in keys])
        for (d, t), (u, w, qg, attn, kdt, egl) in zip(keys, res):
            u_s[d, t] = u
            wq_s[d, t, 0:c, :] = w
            wq_s[d, t, c:2 * c, :] = qg
            at_s[d, t] = attn
            kd_s[d, t] = kdt
            gl_s[d, t] = egl

    def scan_group(g, carry):
        for j in range(group):
            t = g * group + j
            r0s = [pl.multiple_of(chunk_of(d, t) * c, c) for d in range(2)]
            wss = [jnp.dot(wq_s[d, t], carry[d].astype(BF16), preferred_element_type=F32) for d in range(2)]
            yield
            v_news = [(u_s[d, t] - wss[d][0:c, :]).astype(BF16) for d in range(2)]
            nxt = tuple(carry[d] * gl_s[d, t] + jnp.dot(kd_s[d, t], v_news[d], preferred_element_type=F32)
                        for d in range(2))
            for d in range(2):
                o_s[d, pl.ds(r0s[d], c), :] = (wss[d][c:2 * c, :]
                                               + jnp.dot(at_s[d, t], v_news[d], preferred_element_type=F32))
            yield
            carry = nxt
        return carry

    def pipe_body(g, carry):
        carry, _ = _weave(scan_group(g - 1, carry), prep_group(g))
        return carry

    zero = jnp.zeros((HEAD_DIM, HEAD_DIM), F32)
    _weave(prep_group(0))
    carry = lax.fori_loop(1, n_groups, pipe_body, (zero, zero))
    _weave(scan_group(n_groups - 1, carry))

    def finish(lo, n, z_ref, out_ref):
        o = o_s[0, lo:lo + n, :] + o_s[1, lo:lo + n, :]
        out_ref[...] = (_rms(o) * gn_ref[...] * _silu(z_ref[...])).astype(out_ref.dtype)

    finish(n_ctx, seq, zl_ref, ol_ref)
    if need_ctx:
        finish(0, n_ctx, zc_ref, oc_ref)


def gdn_mixer(p, ab, conv_w, a_log, dt_bias, gdn_g, n_batch, seq, n_ctx, need_ctx):
    cb = (n_batch * seq) // n_ctx
    nh = N_HEADS_HALF
    m = p.shape[0]
    abh = ab[:, 0:4 * nh].reshape(m, 4, nh).transpose(2, 0, 1)
    n_chunks = (seq + n_ctx) // GDN_CHUNK

    def lat(col0):
        return pl.BlockSpec((seq, HEAD_DIM), lambda b, h: (b, col0 + h))

    def ctxs(col0):
        return pl.BlockSpec((n_ctx, HEAD_DIM), lambda b, h: (cb + b, col0 + h))

    in_specs = [lat(0), lat(nh), lat(2 * nh), lat(3 * nh),
                pl.BlockSpec((None, seq, 4), lambda b, h: (h, b, 0)),
                ctxs(0), ctxs(nh), ctxs(2 * nh), ctxs(3 * nh),
                pl.BlockSpec((None, n_ctx, 4), lambda b, h: (h, cb + b, 0)),
                pl.BlockSpec((3, HEAD_DIM), lambda b, h: (0, h)),
                pl.BlockSpec((3, HEAD_DIM), lambda b, h: (0, nh + h)),
                pl.BlockSpec((3, HEAD_DIM), lambda b, h: (0, 2 * nh + h)),
                pl.BlockSpec(memory_space=pltpu.SMEM),
                pl.BlockSpec(memory_space=pltpu.SMEM),
                pl.BlockSpec((1, HEAD_DIM), lambda b, h: (0, 0))]
    out_specs = [pl.BlockSpec((seq, HEAD_DIM), lambda b, h: (b, h))]
    out_shape = [jax.ShapeDtypeStruct((n_batch * seq, HALF_W), BF16)]
    if need_ctx:
        out_specs.append(pl.BlockSpec((n_ctx, HEAD_DIM), lambda b, h: (b, h)))
        out_shape.append(jax.ShapeDtypeStruct((n_batch * n_ctx, HALF_W), BF16))
    tot = seq + n_ctx
    outs = pl.pallas_call(
        functools.partial(_gdn_kernel, seq=seq, n_ctx=n_ctx, need_ctx=need_ctx),
        grid=(n_batch, nh),
        in_specs=in_specs,
        out_specs=out_specs,
        out_shape=out_shape,
        scratch_shapes=[pltpu.VMEM((tot, HEAD_DIM), F32), pltpu.VMEM((tot, HEAD_DIM), F32),
                        pltpu.VMEM((tot, HEAD_DIM), F32),
                        pltpu.VMEM((2, tot, HEAD_DIM), F32), pltpu.VMEM((2, tot, HEAD_DIM), F32),
                        pltpu.VMEM((2, n_chunks, GDN_CHUNK, HEAD_DIM), F32), pltpu.VMEM((2, tot, HEAD_DIM), F32),
                        pltpu.VMEM((2, n_chunks, 2 * GDN_CHUNK, HEAD_DIM), BF16),
                        pltpu.VMEM((2, n_chunks, GDN_CHUNK, GDN_CHUNK), BF16),
                        pltpu.VMEM((2, n_chunks, HEAD_DIM, GDN_CHUNK), BF16),
                        pltpu.VMEM((2, n_chunks, 1, HEAD_DIM), F32)],
        compiler_params=_cparams(("arbitrary", "arbitrary")),
        name="gdn",
    )(p, p, p, p, abh, p, p, p, p, abh, conv_w, conv_w, conv_w, a_log, dt_bias, gdn_g.reshape(1, HEAD_DIM))
    return (outs[0], outs[1]) if need_ctx else (outs[0], None)


def _softmax_pv_pair(qs, ks, v_ext):
    ss = _each(_bdot_nt, qs, ks)
    outs = []
    for s in ss:
        e = jnp.exp(s - jnp.max(s, axis=-1, keepdims=True)).astype(BF16)
        r = jnp.dot(e, v_ext, preferred_element_type=F32)
        outs.append(r[:, 0:HEAD_DIM] / r[:, HEAD_DIM:2 * HEAD_DIM])
    return outs


def _diff_kernel(*refs, seq, n_ctx, need_ctx, lambda_init):
    (q_ref, kl_ref, vl_ref, kc_ref, vc_ref, cq_ref, sq_ref, ck_ref, sk_ref,
     lq1_ref, lk1_ref, lq2_ref, lk2_ref, sg_ref) = refs[:14]
    if need_ctx:
        qc_ref, o_ref, oc_ref, k1_s, k2_s, v_s = refs[14:]
    else:
        o_ref, k1_s, k2_s, v_s = refs[14:]
    tot = seq + n_ctx
    qscale = DIFF_DH ** -0.5
    lam = (jnp.exp(jnp.sum(lq1_ref[...] * lk1_ref[...], axis=-1, keepdims=True))
           - jnp.exp(jnp.sum(lq2_ref[...] * lk2_ref[...], axis=-1, keepdims=True)) + lambda_init)

    def combine(qv, lo, hi):
        o1, o2 = _softmax_pv_pair([qv[:, 0:DIFF_DH], qv[:, DIFF_DH:2 * DIFF_DH]],
                                  [k1_s[lo:hi, :], k2_s[lo:hi, :]], v_s[lo:hi, :])
        o = o1 - lam * o2
        return (_rms(o) * sg_ref[...] * (1.0 - lambda_init)).astype(BF16)

    @pl.when(pl.program_id(2) == 0)
    def _():
        k = _rope(kl_ref[...], ck_ref[...], sk_ref[...], DIFF_DH // 4)
        k1_s[0:seq, :] = k[:, 0:DIFF_DH].astype(BF16)
        k2_s[0:seq, :] = k[:, DIFF_DH:2 * DIFF_DH].astype(BF16)
        kc = kc_ref[...]
        k1_s[seq:tot, :] = kc[:, 0:DIFF_DH].astype(BF16)
        k2_s[seq:tot, :] = kc[:, DIFF_DH:2 * DIFF_DH].astype(BF16)
        v_s[0:seq, 0:HEAD_DIM] = vl_ref[...].astype(BF16)
        v_s[seq:tot, 0:HEAD_DIM] = vc_ref[...].astype(BF16)
        v_s[:, HEAD_DIM:2 * HEAD_DIM] = jnp.ones((tot, HEAD_DIM), BF16)
        if need_ctx:
            oc_ref[...] = combine(qc_ref[...] * qscale, seq, tot)

    q = _rope(q_ref[...], cq_ref[...], sq_ref[...], DIFF_DH // 4) * qscale
    o_ref[...] = combine(q, 0, tot)


def diff_mixer(p, cos, sin, lq1, lk1, lq2, lk2, subln_g, lambda_init, n_batch, seq, n_ctx, need_ctx):
    nh = N_HEADS_HALF
    tq = 256
    nq = seq // tq
    cb = (n_batch * seq) // n_ctx
    vec = lambda n: pl.BlockSpec((1, n), lambda b, h, i: (0, 0))
    in_specs = [
        pl.BlockSpec((tq, HEAD_DIM), lambda b, h, i: (b * nq + i, 4 * nh + h)),
        pl.BlockSpec((seq, HEAD_DIM), lambda b, h, i: (b, 5 * nh + h)),
        pl.BlockSpec((seq, HEAD_DIM), lambda b, h, i: (b, 6 * nh + h)),
        pl.BlockSpec((n_ctx, HEAD_DIM), lambda b, h, i: (cb + b, 5 * nh + h)),
        pl.BlockSpec((n_ctx, HEAD_DIM), lambda b, h, i: (cb + b, 6 * nh + h)),
        pl.BlockSpec((tq, HEAD_DIM), lambda b, h, i: (i, 0)),
        pl.BlockSpec((tq, HEAD_DIM), lambda b, h, i: (i, 0)),
        pl.BlockSpec((seq, HEAD_DIM), lambda b, h, i: (0, 0)),
        pl.BlockSpec((seq, HEAD_DIM), lambda b, h, i: (0, 0)),
        vec(DIFF_DH), vec(DIFF_DH), vec(DIFF_DH), vec(DIFF_DH), vec(HEAD_DIM),
    ]
    args = [p, p, p, p, p, cos, sin, cos, sin,
            lq1.reshape(1, -1), lk1.reshape(1, -1), lq2.reshape(1, -1), lk2.reshape(1, -1), subln_g.reshape(1, -1)]
    out_specs = [pl.BlockSpec((tq, HEAD_DIM), lambda b, h, i: (b * nq + i, h))]
    out_shape = [jax.ShapeDtypeStruct((n_batch * seq, HALF_W), BF16)]
    if need_ctx:
        in_specs.append(pl.BlockSpec((n_ctx, HEAD_DIM), lambda b, h, i: (cb + b, 4 * nh + h)))
        args.append(p)
        out_specs.append(pl.BlockSpec((n_ctx, HEAD_DIM), lambda b, h, i: (b, h)))
        out_shape.append(jax.ShapeDtypeStruct((n_batch * n_ctx, HALF_W), BF16))
    tot = seq + n_ctx
    outs = pl.pallas_call(
        functools.partial(_diff_kernel, seq=seq, n_ctx=n_ctx, need_ctx=need_ctx, lambda_init=lambda_init),
        grid=(n_batch, nh, nq),
        in_specs=in_specs,
        out_specs=out_specs,
        out_shape=out_shape,
        scratch_shapes=[pltpu.VMEM((tot, DIFF_DH), BF16), pltpu.VMEM((tot, DIFF_DH), BF16),
                        pltpu.VMEM((tot, 2 * HEAD_DIM), BF16)],
        compiler_params=_cparams(("arbitrary", "arbitrary", "arbitrary")),
        name="diff_attn",
    )(*args)
    return (outs[0], outs[1]) if need_ctx else (outs[0], None)


def _na_bias_table(rpb, n_rows):
    kr = min(NA_ROWS, n_rows)
    c = np.arange(GRID_W)[:, None]
    kc = np.arange(GRID_W)[None, :]
    win_c = np.clip(c - NA_COLS // 2, 0, GRID_W - NA_COLS)
    ok = (kc >= win_c) & (kc < win_c + NA_COLS)
    dc = np.clip(kc - c + NA_COLS - 1, 0, 2 * NA_COLS - 2)
    onehot = (dc[None] == np.arange(2 * NA_COLS - 1)[:, None, None]).astype(np.float32)
    e = jnp.einsum('hdr,rck->hdck', rpb, jnp.asarray(onehot), precision=lax.Precision.HIGHEST)
    e = jnp.where(jnp.asarray(ok)[None, None], e, NEG_BIG)
    tab = jnp.stack([e[:, NA_ROWS - 1 - o:NA_ROWS - 1 - o + kr] for o in range(NA_ROWS)], axis=1)
    return tab.transpose(0, 1, 3, 2, 4).reshape(rpb.shape[0], NA_ROWS, GRID_W, kr * GRID_W)


def _na_kernel(*refs, seq, n_ctx, need_ctx):
    q_ref, k_ref, v_ref, kc_ref, vc_ref, bias_ref = refs[:6]
    if need_ctx:
        qc_ref, o_ref, oc_ref, k_s, v_s, kc_s, vc_s = refs[6:]
    else:
        o_ref, k_s, v_s, kc_s, vc_s = refs[6:]
    n_rows = seq // GRID_W
    kr = min(NA_ROWS, n_rows)
    nl = kr * GRID_W
    scale = HEAD_DIM ** -0.5
    k_s[...] = k_ref[...].astype(BF16)
    v_s[...] = v_ref[...].astype(BF16)
    kc_s[...] = kc_ref[...].astype(BF16)
    vc_s[...] = vc_ref[...].astype(BF16)

    def row_group(it, carry):
        rs = [it * NA_ROW_GROUP + j for j in range(NA_ROW_GROUP)]
        q0s = [pl.multiple_of(r * GRID_W, GRID_W) for r in rs]
        starts = [jnp.clip(r - kr // 2, 0, n_rows - kr) for r in rs]
        k0s = [pl.multiple_of(st * GRID_W, GRID_W) for st in starts]
        qs = [q_ref[pl.ds(q0, GRID_W), :] for q0 in q0s]
        s_locs = _each(lambda q, k0, r, st: _bdot_nt(q, k_s[pl.ds(k0, nl), :]) * scale + bias_ref[r - st],
                       qs, k0s, rs, starts)
        s_ctxs = [_bdot_nt(q, kc_s[...]) * scale for q in qs]
        ms = _each(lambda a, b: jnp.maximum(jnp.max(a, axis=-1, keepdims=True), jnp.max(b, axis=-1, keepdims=True)),
                   s_locs, s_ctxs)
        e_locs = _each(lambda s, m: jnp.exp(s - m), s_locs, ms)
        e_ctxs = _each(lambda s, m: jnp.exp(s - m), s_ctxs, ms)
        dens = _each(lambda a, b: jnp.sum(a, axis=-1, keepdims=True) + jnp.sum(b, axis=-1, keepdims=True),
                     e_locs, e_ctxs)
        os_ = _each(lambda el, ec, k0: _bdot(el, v_s[pl.ds(k0, nl), :]) + _bdot(ec, vc_s[...]), e_locs, e_ctxs, k0s)
        for q0, o, den in zip(q0s, os_, dens):
            o_ref[pl.ds(q0, GRID_W), :] = (o / den).astype(BF16)
        return carry

    lax.fori_loop(0, n_rows // NA_ROW_GROUP, row_group, 0)
    if need_ctx:
        s = _bdot_nt(qc_ref[...], kc_s[...]) * scale
        e = jnp.exp(s - jnp.max(s, axis=-1, keepdims=True))
        oc_ref[...] = (_bdot(e, vc_s[...]) / jnp.sum(e, axis=-1, keepdims=True)).astype(BF16)


def na_mixer(p, bias_tab, n_batch, seq, n_ctx, need_ctx):
    nh = N_HEADS_HALF
    cb = (n_batch * seq) // n_ctx
    nl = bias_tab.shape[-1]
    in_specs = [
        pl.BlockSpec((seq, HEAD_DIM), lambda b, h: (b, h)),
        pl.BlockSpec((seq, HEAD_DIM), lambda b, h: (b, nh + h)),
        pl.BlockSpec((seq, HEAD_DIM), lambda b, h: (b, 2 * nh + h)),
        pl.BlockSpec((n_ctx, HEAD_DIM), lambda b, h: (cb + b, nh + h)),
        pl.BlockSpec((n_ctx, HEAD_DIM), lambda b, h: (cb + b, 2 * nh + h)),
        pl.BlockSpec((None, NA_ROWS, GRID_W, nl), lambda b, h: (h, 0, 0, 0)),
    ]
    args = [p, p, p, p, p, bias_tab]
    out_specs = [pl.BlockSpec((seq, HEAD_DIM), lambda b, h: (b, h))]
    out_shape = [jax.ShapeDtypeStruct((n_batch * seq, HALF_W), BF16)]
    if need_ctx:
        in_specs.append(pl.BlockSpec((n_ctx, HEAD_DIM), lambda b, h: (cb + b, h)))
        args.append(p)
        out_specs.append(pl.BlockSpec((n_ctx, HEAD_DIM), lambda b, h: (b, h)))
        out_shape.append(jax.ShapeDtypeStruct((n_batch * n_ctx, HALF_W), BF16))
    outs = pl.pallas_call(
        functools.partial(_na_kernel, seq=seq, n_ctx=n_ctx, need_ctx=need_ctx),
        grid=(n_batch, nh),
        in_specs=in_specs,
        out_specs=out_specs,
        out_shape=out_shape,
        scratch_shapes=[pltpu.VMEM((seq, HEAD_DIM), BF16), pltpu.VMEM((seq, HEAD_DIM), BF16),
                        pltpu.VMEM((n_ctx, HEAD_DIM), BF16), pltpu.VMEM((n_ctx, HEAD_DIM), BF16)],
        compiler_params=_cparams(("arbitrary", "arbitrary")),
        name="na_attn",
    )(*args)
    return (outs[0], outs[1]) if need_ctx else (outs[0], None)


def _win_kernel(*refs, seq, n_ctx, need_ctx):
    q_ref, k_ref, v_ref, kc_ref, vc_ref, cos_ref, sin_ref, sink_ref = refs[:8]
    if need_ctx:
        qc_ref, o_ref, oc_ref, k_s, v_s, kc_s, vc_s = refs[8:]
    else:
        o_ref, k_s, v_s, kc_s, vc_s = refs[8:]
    kvh = pl.program_id(1)
    wb = WIN_BLOCK
    band = 3 * wb
    scale = HEAD_DIM ** -0.5
    k_s[...] = _rope(k_ref[...], cos_ref[...], sin_ref[...], HEAD_DIM // 4).astype(BF16)
    v_s[...] = v_ref[...].astype(BF16)
    kc_s[...] = kc_ref[...].astype(BF16)
    vc_s[...] = vc_ref[...].astype(BF16)
    sinks = [jnp.full((1, 1), sink_ref[kvh * WIN_GROUP + j], F32) for j in range(WIN_GROUP)]

    def attend(qs, k_loc, v_loc, valid):
        rowmax = lambda s: jnp.max(s, axis=-1, keepdims=True)
        rowsum = lambda e: jnp.sum(e, axis=-1, keepdims=True)
        s_ctxs = [_bdot_nt(q, kc_s[...]) * scale for q in qs]
        ms = _each(lambda s, sk: jnp.maximum(rowmax(s), sk), s_ctxs, sinks)
        if k_loc is not None:
            s_locs = [jnp.where(valid, _bdot_nt(q, k_loc) * scale, NEG_BIG) for q in qs]
            ms = _each(lambda m, s: jnp.maximum(m, rowmax(s)), ms, s_locs)
        e_ctxs = _each(lambda s, m: jnp.exp(s - m), s_ctxs, ms)
        dens = _each(lambda e, sk, m: rowsum(e) + jnp.exp(sk - m), e_ctxs, sinks, ms)
        os_ = [_bdot(e, vc_s[...]) for e in e_ctxs]
        if k_loc is not None:
            e_locs = _each(lambda s, m: jnp.exp(s - m), s_locs, ms)
            dens = _each(lambda den, e: den + rowsum(e), dens, e_locs)
            os_ = _each(lambda o, e: o + _bdot(e, v_loc), os_, e_locs)
        return _each(lambda o, den: (o / den).astype(BF16), os_, dens)

    def q_block(n, carry):
        q0 = pl.multiple_of(n * wb, wb)
        k0 = pl.multiple_of(jnp.clip(n * wb - wb, 0, seq - band), wb)
        qpos = q0 + lax.broadcasted_iota(jnp.int32, (wb, band), 0)
        kpos = k0 + lax.broadcasted_iota(jnp.int32, (wb, band), 1)
        valid = jnp.abs(qpos - kpos) <= WIN
        k_loc = k_s[pl.ds(k0, band), :]
        v_loc = v_s[pl.ds(k0, band), :]
        cos = cos_ref[pl.ds(q0, wb), :]
        sin = sin_ref[pl.ds(q0, wb), :]
        qs = [_rope(q_ref[pl.ds(q0, wb), j * HEAD_DIM:(j + 1) * HEAD_DIM], cos, sin, HEAD_DIM // 4)
              for j in range(WIN_GROUP)]
        for j, o in enumerate(attend(qs, k_loc, v_loc, valid)):
            o_ref[pl.ds(q0, wb), j * HEAD_DIM:(j + 1) * HEAD_DIM] = o
        return carry

    lax.fori_loop(0, seq // wb, q_block, 0)
    if need_ctx:
        qs = [qc_ref[:, j * HEAD_DIM:(j + 1) * HEAD_DIM] for j in range(WIN_GROUP)]
        for j, o in enumerate(attend(qs, None, None, None)):
            oc_ref[:, j * HEAD_DIM:(j + 1) * HEAD_DIM] = o


def win_mixer(p, cos, sin, sink, n_batch, seq, n_ctx, need_ctx):
    nh = N_HEADS_HALF
    cb = (n_batch * seq) // n_ctx
    gw = WIN_GROUP * HEAD_DIM
    q_blk0 = (3 * nh * HEAD_DIM) // gw
    in_specs = [
        pl.BlockSpec((seq, gw), lambda b, g: (b, q_blk0 + g)),
        pl.BlockSpec((seq, HEAD_DIM), lambda b, g: (b, 4 * nh + g)),
        pl.BlockSpec((seq, HEAD_DIM), lambda b, g: (b, 4 * nh + WIN_KV_HEADS + g)),
        pl.BlockSpec((n_ctx, HEAD_DIM), lambda b, g: (cb + b, 4 * nh + g)),
        pl.BlockSpec((n_ctx, HEAD_DIM), lambda b, g: (cb + b, 4 * nh + WIN_KV_HEADS + g)),
        pl.BlockSpec((seq, HEAD_DIM), lambda b, g: (0, 0)),
        pl.BlockSpec((seq, HEAD_DIM), lambda b, g: (0, 0)),
        pl.BlockSpec(memory_space=pltpu.SMEM),
    ]
    args = [p, p, p, p, p, cos, sin, sink]
    out_specs = [pl.BlockSpec((seq, gw), lambda b, g: (b, g))]
    out_shape = [jax.ShapeDtypeStruct((n_batch * seq, HALF_W), BF16)]
    if need_ctx:
        in_specs.append(pl.BlockSpec((n_ctx, gw), lambda b, g: (cb + b, q_blk0 + g)))
        args.append(p)
        out_specs.append(pl.BlockSpec((n_ctx, gw), lambda b, g: (b, g)))
        out_shape.append(jax.ShapeDtypeStruct((n_batch * n_ctx, HALF_W), BF16))
    outs = pl.pallas_call(
        functools.partial(_win_kernel, seq=seq, n_ctx=n_ctx, need_ctx=need_ctx),
        grid=(n_batch, WIN_KV_HEADS),
        in_specs=in_specs,
        out_specs=out_specs,
        out_shape=out_shape,
        scratch_shapes=[pltpu.VMEM((seq, HEAD_DIM), BF16), pltpu.VMEM((seq, HEAD_DIM), BF16),
                        pltpu.VMEM((n_ctx, HEAD_DIM), BF16), pltpu.VMEM((n_ctx, HEAD_DIM), BF16)],
        compiler_params=_cparams(("arbitrary", "arbitrary")),
        name="win_attn",
    )(*args)
    return (outs[0], outs[1]) if need_ctx else (outs[0], None)


def kernel(x, c, ctx, c_ctx, ada_w, ada_b, norm_mix_g, norm_ffn_g, w_in_even, gdn_conv_w, gdn_a_log, gdn_dt_bias,
           gdn_norm_g, diff_lambda_q1, diff_lambda_k1, diff_lambda_q2, diff_lambda_k2, diff_subln_g, w_in_odd,
           na_rpb, win_sink, w_out, ffn_w_gate, ffn_w_up, ffn_w_down, final_norm_g):
    n_batch, seq, d = x.shape
    n_ctx = ctx.shape[1]
    n_lat = n_batch * seq
    assert n_batch + 1 <= MOD_ROWS and seq % GRID_W == 0
    hf = jnp.concatenate([x.reshape(n_lat, d), ctx.reshape(n_batch * n_ctx, d)], axis=0)
    cc = jnp.zeros((MOD_ROWS, d), F32).at[:n_batch].set(c).at[n_batch].set(c_ctx)
    mods = ada_mod(cc, ada_w, ada_b)

    cos64, sin64 = _rope_tables(seq, DIFF_DH)
    cos128, sin128 = _rope_tables(seq, HEAD_DIM)

    qkvz = 4 * HALF_W
    for l in range(DEPTH):
        need_ctx = l < DEPTH - 1
        i = l // 2
        if l % 2 == 0:
            w = w_in_even[i]
            w_main = jnp.concatenate([w[:, :qkvz], w[:, qkvz + 4 * N_HEADS_HALF:]], axis=1).astype(BF16)
            w_ab = jnp.pad(w[:, qkvz:qkvz + 4 * N_HEADS_HALF], ((0, 0), (0, HEAD_DIM - 4 * N_HEADS_HALF))).astype(BF16)
            p, ab = in_proj(hf, mods[l], norm_mix_g[l], w_main, w_ab, n_lat, seq, 1024)
            lambda_init = 0.8 - 0.6 * float(np.exp(-0.3 * l))
            ya = gdn_mixer(p, ab, gdn_conv_w[i], gdn_a_log[i], gdn_dt_bias[i], gdn_norm_g[i],
                           n_batch, seq, n_ctx, need_ctx)
            yb = diff_mixer(p, cos64, sin64, diff_lambda_q1[i], diff_lambda_k1[i], diff_lambda_q2[i],
                            diff_lambda_k2[i], diff_subln_g[i], lambda_init, n_batch, seq, n_ctx, need_ctx)
        else:
            p = in_proj(hf, mods[l], norm_mix_g[l], w_in_odd[i].astype(BF16), None, n_lat, seq, 1536)
            bias_tab = _na_bias_table(na_rpb[i], seq // GRID_W)
            ya = na_mixer(p, bias_tab, n_batch, seq, n_ctx, need_ctx)
            yb = win_mixer(p, cos128, sin128, win_sink[i], n_batch, seq, n_ctx, need_ctx)
        y_ctx = (ya[1], yb[1]) if need_ctx else None
        h1, u2 = out_proj((ya[0], yb[0]), y_ctx, hf, mods[l], norm_ffn_g[l], w_out[l].astype(BF16), n_lat, seq)
        hf = ffn(u2, h1, mods[l], ffn_w_gate[l].astype(BF16), ffn_w_up[l].astype(BF16), ffn_w_down[l].astype(BF16),
                 final_norm_g, n_lat, seq, final=not need_ctx)
    return hf[:n_lat].reshape(n_batch, seq, d)
```

```python
import functools

import jax
import jax.numpy as jnp
import numpy as np
from jax import lax
from jax.experimental import pallas as pl
from jax.experimental.pallas import tpu as pltpu

F32 = jnp.float32
BF16 = jnp.bfloat16

DEPTH = 4
GRID_W = 64
HEAD_DIM = 128
N_HEADS_HALF = 8
HALF_W = N_HEADS_HALF * HEAD_DIM
GDN_CHUNK = 64
DIFF_DH = 64
NA_ROWS = 8
NA_COLS = 16
NA_ROW_GROUP = 8
WIN = 128
WIN_BLOCK = 128
WIN_KV_HEADS = 2
WIN_GROUP = N_HEADS_HALF // WIN_KV_HEADS
ROPE_BASE = 10000.0
NORM_EPS = 1e-6
LOG2E = 1.4426950408889634
NEG_BIG = -1e30

V7X_VMEM_LIMIT = 56 * 1024 * 1024
MOD_ROWS = 16


def _cparams(sem):
    return pltpu.CompilerParams(dimension_semantics=sem, vmem_limit_bytes=V7X_VMEM_LIMIT)


def _bdot(a, b):
    return jnp.dot(a.astype(BF16), b.astype(BF16), preferred_element_type=F32)


def _bdot_nt(a, b):
    return lax.dot_general(a.astype(BF16), b.astype(BF16), (((1,), (1,)), ((), ())),
                           preferred_element_type=F32)


def _bdot_tn(a, b):
    return lax.dot_general(a.astype(BF16), b.astype(BF16), (((0,), (0,)), ((), ())),
                           preferred_element_type=F32)


def _hdot(a, b):
    return jnp.dot(a, b, preferred_element_type=F32, precision=lax.Precision.HIGHEST)


def _silu(x):
    return x * jax.nn.sigmoid(x)


def _rms(x):
    return x * lax.rsqrt(jnp.mean(x * x, axis=-1, keepdims=True) + NORM_EPS)


def _row_tile(n_lat, n_ctx, cap):
    for t in (1024, 512, 256):
        if t <= cap and n_lat % t == 0 and n_ctx % t == 0:
            return t
    raise ValueError("row counts must be multiples of 256")


def _ada_kernel(c_ref, w_ref, b_ref, o_ref):
    o_ref[...] = _bdot(_silu(c_ref[...]), w_ref[...]) + b_ref[...]


def ada_mod(cc, ada_w, ada_b):
    depth, d, n6 = ada_w.shape
    tn = 1024
    out = pl.pallas_call(
        _ada_kernel,
        grid=(depth, n6 // tn),
        in_specs=[
            pl.BlockSpec((MOD_ROWS, d), lambda l, j: (0, 0)),
            pl.BlockSpec((None, d, tn), lambda l, j: (l, 0, j)),
            pl.BlockSpec((None, 1, tn), lambda l, j: (l, 0, j)),
        ],
        out_specs=pl.BlockSpec((None, MOD_ROWS, tn), lambda l, j: (l, 0, j)),
        out_shape=jax.ShapeDtypeStruct((depth, MOD_ROWS, n6), F32),
        compiler_params=_cparams(("arbitrary", "arbitrary")),
        name="ada_mod",
    )(cc, ada_w, ada_b.reshape(depth, 1, n6))
    return out.reshape(depth, MOD_ROWS, 6, d)


def _norm_mod(x, g, shift, scale):
    return (_rms(x) * g) * (1.0 + scale) + shift


def _seg_fn(n_lat, seq, tm):
    n_lat_tiles = n_lat // tm
    n_batch = n_lat // seq
    return lambda i: jnp.where(i < n_lat_tiles, (i * tm) // seq, n_batch)


def _inproj_kernel(h_ref, mod_ref, g_ref, w_ref, *rest, has_aux):
    if has_aux:
        waux_ref, o_ref, oaux_ref, u_scr = rest
    else:
        o_ref, u_scr = rest

    @pl.when(pl.program_id(1) == 0)
    def _():
        u = _norm_mod(h_ref[...], g_ref[...], mod_ref[0:1, :], mod_ref[1:2, :])
        u_scr[...] = u.astype(BF16)
        if has_aux:
            oaux_ref[...] = jnp.dot(u_scr[...], waux_ref[...], preferred_element_type=F32)

    o_ref[...] = jnp.dot(u_scr[...], w_ref[...], preferred_element_type=F32)


def in_proj(hf, mod_l, g, w, w_aux, n_lat, seq, tn):
    m, d = hf.shape
    n = w.shape[1]
    tm = _row_tile(n_lat, m - n_lat, 1024)
    seg = _seg_fn(n_lat, seq, tm)
    has_aux = w_aux is not None
    in_specs = [
        pl.BlockSpec((tm, d), lambda i, j: (i, 0)),
        pl.BlockSpec((None, 6, d), lambda i, j: (seg(i), 0, 0)),
        pl.BlockSpec((1, d), lambda i, j: (0, 0)),
        pl.BlockSpec((d, tn), lambda i, j: (0, j)),
    ]
    out_specs = [pl.BlockSpec((tm, tn), lambda i, j: (i, j))]
    out_shape = [jax.ShapeDtypeStruct((m, n), F32)]
    args = [hf, mod_l, g.reshape(1, d), w]
    if has_aux:
        in_specs.append(pl.BlockSpec((d, HEAD_DIM), lambda i, j: (0, 0)))
        out_specs.append(pl.BlockSpec((tm, HEAD_DIM), lambda i, j: (i, 0)))
        out_shape.append(jax.ShapeDtypeStruct((m, HEAD_DIM), F32))
        args.append(w_aux)
    outs = pl.pallas_call(
        functools.partial(_inproj_kernel, has_aux=has_aux),
        grid=(m // tm, n // tn),
        in_specs=in_specs,
        out_specs=out_specs,
        out_shape=out_shape,
        scratch_shapes=[pltpu.VMEM((tm, d), BF16)],
        compiler_params=_cparams(("arbitrary", "arbitrary")),
        name="in_proj",
    )(*args)
    return outs if has_aux else outs[0]


def _outproj_kernel(*refs, n_lat_tiles, has_ctx):
    if has_ctx:
        yal_ref, ybl_ref, yac_ref, ybc_ref, h_ref, mod_ref, g_ref, w_ref, ho_ref, u_ref = refs
        is_ctx = pl.program_id(0) >= n_lat_tiles
        ya = jnp.where(is_ctx, yac_ref[...], yal_ref[...])
        yb = jnp.where(is_ctx, ybc_ref[...], ybl_ref[...])
    else:
        yal_ref, ybl_ref, h_ref, mod_ref, g_ref, w_ref, ho_ref, u_ref = refs
        ya, yb = yal_ref[...], ybl_ref[...]
    y = (jnp.dot(ya, w_ref[0:HALF_W, :], preferred_element_type=F32)
         + jnp.dot(yb, w_ref[HALF_W:2 * HALF_W, :], preferred_element_type=F32))
    hn = h_ref[...] + mod_ref[2:3, :] * y
    ho_ref[...] = hn
    u_ref[...] = _norm_mod(hn, g_ref[...], mod_ref[3:4, :], mod_ref[4:5, :]).astype(BF16)


def out_proj(y_lat, y_ctx, hf, mod_l, g, w, n_lat, seq):
    m, d = hf.shape
    has_ctx = y_ctx is not None
    m_out = m if has_ctx else n_lat
    tm = _row_tile(n_lat, m - n_lat, 512)
    n_lat_tiles = n_lat // tm
    seg = _seg_fn(n_lat, seq, tm)
    lat_spec = pl.BlockSpec((tm, HALF_W), lambda i: (jnp.minimum(i, n_lat_tiles - 1), 0))
    ctx_spec = pl.BlockSpec((tm, HALF_W), lambda i: (jnp.maximum(i - n_lat_tiles, 0), 0))
    in_specs = [lat_spec, lat_spec] + ([ctx_spec, ctx_spec] if has_ctx else []) + [
        pl.BlockSpec((tm, d), lambda i: (i, 0)),
        pl.BlockSpec((None, 6, d), lambda i: (seg(i), 0, 0)),
        pl.BlockSpec((1, d), lambda i: (0, 0)),
        pl.BlockSpec((d, d), lambda i: (0, 0)),
    ]
    args = list(y_lat) + (list(y_ctx) if has_ctx else []) + [hf, mod_l, g.reshape(1, d), w]
    return pl.pallas_call(
        functools.partial(_outproj_kernel, n_lat_tiles=n_lat_tiles, has_ctx=has_ctx),
        grid=(m_out // tm,),
        in_specs=in_specs,
        out_specs=[pl.BlockSpec((tm, d), lambda i: (i, 0)), pl.BlockSpec((tm, d), lambda i: (i, 0))],
        out_shape=[jax.ShapeDtypeStruct((m_out, d), F32), jax.ShapeDtypeStruct((m_out, d), BF16)],
        compiler_params=_cparams(("arbitrary",)),
        name="out_proj",
    )(*args)


def _ffn_kernel(u_ref, h_ref, mod_ref, wg_ref, wu_ref, wd_ref, fg_ref, o_ref, acc_ref, *, final):
    j = pl.program_id(1)

    @pl.when(j == 0)
    def _():
        acc_ref[...] = jnp.zeros_like(acc_ref)

    u = u_ref[...]
    gate = jnp.dot(u, wg_ref[...], preferred_element_type=F32)
    up = jnp.dot(u, wu_ref[...], preferred_element_type=F32)
    act = (_silu(gate) * up).astype(BF16)
    acc_ref[...] = acc_ref[...] + jnp.dot(act, wd_ref[...], preferred_element_type=F32)

    @pl.when(j == pl.num_programs(1) - 1)
    def _():
        hn = h_ref[...] + mod_ref[5:6, :] * acc_ref[...]
        if final:
            hn = _rms(hn) * fg_ref[...]
        o_ref[...] = hn


def ffn(u2, h1, mod_l, wg, wu, wd, final_g, n_lat, seq, final):
    m, d = h1.shape
    hidden = wg.shape[1]
    tm = _row_tile(n_lat, m - n_lat, 512)
    th = 512
    seg = _seg_fn(n_lat, seq, tm)
    return pl.pallas_call(
        functools.partial(_ffn_kernel, final=final),
        grid=(m // tm, hidden // th),
        in_specs=[
            pl.BlockSpec((tm, d), lambda i, j: (i, 0)),
            pl.BlockSpec((tm, d), lambda i, j: (i, 0)),
            pl.BlockSpec((None, 6, d), lambda i, j: (seg(i), 0, 0)),
            pl.BlockSpec((d, th), lambda i, j: (0, j)),
            pl.BlockSpec((d, th), lambda i, j: (0, j)),
            pl.BlockSpec((th, d), lambda i, j: (j, 0)),
            pl.BlockSpec((1, d), lambda i, j: (0, 0)),
        ],
        out_specs=pl.BlockSpec((tm, d), lambda i, j: (i, 0)),
        out_shape=jax.ShapeDtypeStruct((m, d), F32),
        scratch_shapes=[pltpu.VMEM((tm, d), F32)],
        compiler_params=_cparams(("arbitrary", "arbitrary")),
        name="ffn",
    )(u2, h1, mod_l, wg, wu, wd, final_g.reshape(1, d))


def _ffn_act_kernel(u_ref, wg_ref, wu_ref, o_ref):
    u = u_ref[...]
    gate = jnp.dot(u, wg_ref[...], preferred_element_type=F32)
    up = jnp.dot(u, wu_ref[...], preferred_element_type=F32)
    o_ref[...] = (_silu(gate) * up).astype(BF16)


def _ffn_down_kernel(a_ref, h_ref, mod_ref, wd_ref, o_ref):
    o_ref[...] = h_ref[...] + mod_ref[5:6, :] * jnp.dot(a_ref[...], wd_ref[...], preferred_element_type=F32)


def _final_norm_kernel(h_ref, g_ref, o_ref):
    o_ref[...] = _rms(h_ref[...]) * g_ref[...]


def ffn_split(u2, h1, mod_l, wg, wu, wd, n_lat, seq):
    m, d = h1.shape
    hidden = wg.shape[1]
    tm = _row_tile(n_lat, m - n_lat, 1024)
    tn = 512
    seg = _seg_fn(n_lat, seq, tm)
    act = pl.pallas_call(
        _ffn_act_kernel,
        grid=(m // tm, hidden // tn),
        in_specs=[
            pl.BlockSpec((tm, d), lambda i, j: (i, 0)),
            pl.BlockSpec((d, tn), lambda i, j: (0, j)),
            pl.BlockSpec((d, tn), lambda i, j: (0, j)),
        ],
        out_specs=pl.BlockSpec((tm, tn), lambda i, j: (i, j)),
        out_shape=jax.ShapeDtypeStruct((m, hidden), BF16),
        compiler_params=_cparams(("arbitrary", "arbitrary")),
        name="ffn_act",
    )(u2, wg, wu)
    return pl.pallas_call(
        _ffn_down_kernel,
        grid=(m // tm, d // tn),
        in_specs=[
            pl.BlockSpec((tm, hidden), lambda i, j: (i, 0)),
            pl.BlockSpec((tm, tn), lambda i, j: (i, j)),
            pl.BlockSpec((None, 6, tn), lambda i, j: (seg(i), 0, j)),
            pl.BlockSpec((hidden, tn), lambda i, j: (0, j)),
        ],
        out_specs=pl.BlockSpec((tm, tn), lambda i, j: (i, j)),
        out_shape=jax.ShapeDtypeStruct((m, d), F32),
        compiler_params=_cparams(("arbitrary", "arbitrary")),
        name="ffn_down",
    )(act, h1, mod_l, wd)


def final_norm(h, g):
    m, d = h.shape
    tm = 512
    return pl.pallas_call(
        _final_norm_kernel,
        grid=(m // tm,),
        in_specs=[pl.BlockSpec((tm, d), lambda i: (i, 0)), pl.BlockSpec((1, d), lambda i: (0, 0))],
        out_specs=pl.BlockSpec((tm, d), lambda i: (i, 0)),
        out_shape=jax.ShapeDtypeStruct((m, d), F32),
        compiler_params=_cparams(("arbitrary",)),
        name="final_norm",
    )(h, g.reshape(1, d))


def _rope_tables(seq, comp):
    pos = np.arange(seq)
    rows, cols = pos // GRID_W, pos % GRID_W
    n = comp // 2
    inv = (ROPE_BASE ** (-np.arange(0, n, 2, dtype=np.float32) / n)).astype(np.float32)
    ang_r = rows.astype(np.float32)[:, None] * inv[None, :]
    ang_c = cols.astype(np.float32)[:, None] * inv[None, :]
    ang = jnp.asarray(np.concatenate([ang_r, ang_r, ang_c, ang_c], axis=1))
    sign = np.concatenate([-np.ones(n // 2), np.ones(n // 2)] * 2).astype(np.float32)
    reps = HEAD_DIM // comp
    return jnp.tile(jnp.cos(ang), (1, reps)), jnp.tile(jnp.sin(ang) * jnp.asarray(sign)[None, :], (1, reps))


def _rope(x, cos, sin, half):
    lane = lax.broadcasted_iota(jnp.int32, x.shape, 1)
    first = (lane % (2 * half)) < half
    fwd = pltpu.roll(x, HEAD_DIM - half, 1)
    bwd = pltpu.roll(x, half, 1)
    return x * cos + jnp.where(first, fwd, bwd) * sin


def _shift_rows(x, up):
    t = x.shape[0]
    row = lax.broadcasted_iota(jnp.int32, x.shape, 0)
    if up:
        return jnp.where(row == t - 1, 0.0, pltpu.roll(x, t - 1, 0))
    return jnp.where(row == 0, 0.0, pltpu.roll(x, 1, 0))


def _conv_silu(x, w):
    y = _shift_rows(x, False) * w[0:1, :] + x * w[1:2, :] + _shift_rows(x, True) * w[2:3, :]
    return _silu(y)


def _l2n(x):
    return x * lax.rsqrt(jnp.sum(x * x, axis=-1, keepdims=True) + NORM_EPS)


def _softplus(x):
    return jnp.maximum(x, 0.0) + jnp.log1p(jnp.exp(-jnp.abs(x)))


TRI_BASE = 8
GDN_PREP_GROUPS = (6, 4, 2, 1)


def _each(fn, *lists):
    return [fn(*xs) for xs in zip(*lists)]


def _weave(*gens):
    results = [None] * len(gens)
    live = list(range(len(gens)))
    while live:
        for i in list(live):
            try:
                next(gens[i])
            except StopIteration as stop:
                results[i] = stop.value
                live.remove(i)
    return results


def _unit_tri_inverse_stages(mats):
    c = mats[0].shape[0]
    row = lax.broadcasted_iota(jnp.int32, (c, c), 0)
    col = lax.broadcasted_iota(jnp.int32, (c, c), 1)
    same = lambda n: (row // n) == (col // n)
    eye = (row == col).astype(F32)
    xs = [jnp.where(same(TRI_BASE), -a, 0.0) for a in mats]
    ts = [eye + x for x in xs]
    p = 2
    while p < TRI_BASE:
        xs = [_bdot(x, x) for x in xs]
        yield
        ts = _each(lambda t, x: t + _bdot(t, x), ts, xs)
        yield
        p *= 2
    n = 2 * TRI_BASE
    while n <= c:
        off = same(n) & jnp.logical_not(same(n // 2))
        tas = _each(lambda t, a: _bdot(t, jnp.where(off, a, 0.0)), ts, mats)
        yield
        ts = _each(lambda t, ta: t - _bdot(ta, t), ts, tas)
        yield
        n *= 2
    return ts


def _seg_cumsum(x, rev):
    t = x.shape[0]
    pos = lax.broadcasted_iota(jnp.int32, x.shape, 0) % GDN_CHUNK
    s = 1
    while s < GDN_CHUNK:
        if rev:
            x = x + jnp.where(pos < GDN_CHUNK - s, pltpu.roll(x, t - s, 0), 0.0)
        else:
            x = x + jnp.where(pos >= s, pltpu.roll(x, s, 0), 0.0)
        s *= 2
    return x


def _gdn_prep_stages(qs, ks, vs, gcs, betas, revs):
    c = qs[0].shape[0]
    row = lax.broadcasted_iota(jnp.int32, (c, c), 0)
    col = lax.broadcasted_iota(jnp.int32, (c, c), 1)
    tri = {False: row >= col, True: row <= col}
    strict = {False: row > col, True: row < col}
    decays = _each(lambda gc, rev: jnp.exp(jnp.where(tri[rev], gc[:, 0:c] - jnp.transpose(gc)[0:c, :], -jnp.inf)),
                   gcs, revs)
    kbs = _each(lambda k, b: k * b, ks, betas)
    kks = _each(_bdot_nt, kbs, ks)
    qsc = [q * (HEAD_DIM ** -0.5) for q in qs]
    qks = _each(_bdot_nt, qsc, ks)
    yield
    a_mats = _each(lambda kk, dec, rev: jnp.where(strict[rev], kk * dec, 0.0), kks, decays, revs)
    t_mats = yield from _unit_tri_inverse_stages(a_mats)
    egs = [jnp.exp(gc) for gc in gcs]
    rhs = _each(lambda v, b, kb, eg: jnp.concatenate([v * b, kb * eg], axis=1), vs, betas, kbs, egs)
    uws = _each(_bdot, t_mats, rhs)
    yield
    attns = _each(lambda qk, dec, rev: jnp.where(tri[rev], qk * dec, 0.0).astype(BF16), qks, decays, revs)
    g_lasts = _each(lambda gc, rev: gc[0:1, :] if rev else gc[c - 1:c, :], gcs, revs)
    kdts = _each(lambda k, gl, gc: jnp.transpose(k * jnp.exp(gl - gc)).astype(BF16), ks, g_lasts, gcs)
    return [(uw[:, 0:HEAD_DIM], uw[:, HEAD_DIM:].astype(BF16), (q * eg).astype(BF16), attn, kdt, jnp.exp(gl))
            for uw, q, eg, attn, kdt, gl in zip(uws, qsc, egs, attns, kdts, g_lasts)]


def _gdn_kernel(*refs, seq, n_ctx, need_ctx):
    (ql_ref, kl_ref, vl_ref, zl_ref, abl_ref, qc_ref, kc_ref, vc_ref, zc_ref, abc_ref,
     cwq_ref, cwk_ref, cwv_ref, alog_ref, dtb_ref, gn_ref) = refs[:16]
    if need_ctx:
        ol_ref, oc_ref = refs[16:18]
        scr = refs[18:]
    else:
        ol_ref = refs[16]
        scr = refs[17:]
    q_s, k_s, v_s, gc_s, bt_s, u_s, o_s, wq_s, at_s, kd_s, gl_s = scr
    hd = pl.program_id(1)
    c = GDN_CHUNK
    for (src_q, src_k, src_v, lo, n) in ((qc_ref, kc_ref, vc_ref, 0, n_ctx), (ql_ref, kl_ref, vl_ref, n_ctx, seq)):
        q_s[lo:lo + n, :] = _l2n(_conv_silu(src_q[...], cwq_ref[...]))
        k_s[lo:lo + n, :] = _l2n(_conv_silu(src_k[...], cwk_ref[...]))
        v_s[lo:lo + n, :] = _conv_silu(src_v[...], cwv_ref[...])
    lane4 = lax.broadcasted_iota(jnp.int32, (1, 4), 1)
    neg_decay = -jnp.exp(jnp.where(lane4 == 0, alog_ref[0, hd], alog_ref[1, hd]))
    dt_b = jnp.where(lane4 == 0, dtb_ref[0, hd], dtb_ref[1, hd])
    for (src, lo, n) in ((abc_ref, 0, n_ctx), (abl_ref, n_ctx, seq)):
        ab = src[...]
        g = neg_decay * _softplus(ab + dt_b)
        beta = jax.nn.sigmoid(ab)
        for d in range(2):
            gc_s[d, lo:lo + n, :] = jnp.broadcast_to(_seg_cumsum(g, d == 1)[:, d:d + 1], (n, HEAD_DIM))
            bt_s[d, lo:lo + n, :] = jnp.broadcast_to(beta[:, 2 + d:3 + d], (n, HEAD_DIM))

    n_chunks = (n_ctx + seq) // c
    n_cc = n_ctx // c

    group = next(g for g in GDN_PREP_GROUPS if n_chunks % g == 0)
    n_groups = n_chunks // group

    def chunk_of(d, t):
        return t if d == 0 else jnp.where(t < n_cc, n_cc - 1 - t, n_chunks - 1 + n_cc - t)

    def prep_group(g):
        keys = [(d, g * group + j) for j in range(group) for d in range(2)]
        r0s = [pl.multiple_of(chunk_of(d, t) * c, c) for d, t in keys]
        res = yield from _gdn_prep_stages(
            [q_s[pl.ds(r0, c), :] for r0 in r0s], [k_s[pl.ds(r0, c), :] for r0 in r0s],
            [v_s[pl.ds(r0, c), :] for r0 in r0s], [gc_s[d, pl.ds(r0, c), :] for (d, _), r0 in zip(keys, r0s)],
            [bt_s[d, pl.ds(r0, c), :] for (d, _), r0 in zip(keys, r0s)], [d == 1 for d, _ in keys])
        for (d, t), (u, w, qg, attn, kdt, egl) in zip(keys, res):
            u_s[d, t] = u
            wq_s[d, t, 0:c, :] = w
            wq_s[d, t, c:2 * c, :] = qg
            at_s[d, t] = attn
            kd_s[d, t] = kdt
            gl_s[d, t] = egl

    def scan_group(g, carry):
        for j in range(group):
            t = g * group + j
            r0s = [pl.multiple_of(chunk_of(d, t) * c, c) for d in range(2)]
            wss = [jnp.dot(wq_s[d, t], carry[d].astype(BF16), preferred_element_type=F32) for d in range(2)]
            yield
            v_news = [(u_s[d, t] - wss[d][0:c, :]).astype(BF16) for d in range(2)]
            nxt = tuple(carry[d] * gl_s[d, t] + jnp.dot(kd_s[d, t], v_news[d], preferred_element_type=F32)
                        for d in range(2))
            for d in range(2):
                o_s[d, pl.ds(r0s[d], c), :] = (wss[d][c:2 * c, :]
                                               + jnp.dot(at_s[d, t], v_news[d], preferred_element_type=F32))
            yield
            carry = nxt
        return carry

    def pipe_body(g, carry):
        carry, _ = _weave(scan_group(g - 1, carry), prep_group(g))
        return carry

    zero = jnp.zeros((HEAD_DIM, HEAD_DIM), F32)
    _weave(prep_group(0))
    carry = lax.fori_loop(1, n_groups, pipe_body, (zero, zero))
    _weave(scan_group(n_groups - 1, carry))

    def finish(lo, n, z_ref, out_ref):
        o = o_s[0, lo:lo + n, :] + o_s[1, lo:lo + n, :]
        out_ref[...] = (_rms(o) * gn_ref[...] * _silu(z_ref[...])).astype(out_ref.dtype)

    finish(n_ctx, seq, zl_ref, ol_ref)
    if need_ctx:
        finish(0, n_ctx, zc_ref, oc_ref)


def gdn_mixer(p, ab, conv_w, a_log, dt_bias, gdn_g, n_batch, seq, n_ctx, need_ctx):
    cb = (n_batch * seq) // n_ctx
    nh = N_HEADS_HALF
    m = p.shape[0]
    abh = ab[:, 0:4 * nh].reshape(m, 4, nh).transpose(2, 0, 1)
    n_chunks = (seq + n_ctx) // GDN_CHUNK

    def lat(col0):
        return pl.BlockSpec((seq, HEAD_DIM), lambda b, h: (b, col0 + h))

    def ctxs(col0):
        return pl.BlockSpec((n_ctx, HEAD_DIM), lambda b, h: (cb + b, col0 + h))

    in_specs = [lat(0), lat(nh), lat(2 * nh), lat(3 * nh),
                pl.BlockSpec((None, seq, 4), lambda b, h: (h, b, 0)),
                ctxs(0), ctxs(nh), ctxs(2 * nh), ctxs(3 * nh),
                pl.BlockSpec((None, n_ctx, 4), lambda b, h: (h, cb + b, 0)),
                pl.BlockSpec((3, HEAD_DIM), lambda b, h: (0, h)),
                pl.BlockSpec((3, HEAD_DIM), lambda b, h: (0, nh + h)),
                pl.BlockSpec((3, HEAD_DIM), lambda b, h: (0, 2 * nh + h)),
                pl.BlockSpec(memory_space=pltpu.SMEM),
                pl.BlockSpec(memory_space=pltpu.SMEM),
                pl.BlockSpec((1, HEAD_DIM), lambda b, h: (0, 0))]
    out_specs = [pl.BlockSpec((seq, HEAD_DIM), lambda b, h: (b, h))]
    out_shape = [jax.ShapeDtypeStruct((n_batch * seq, HALF_W), BF16)]
    if need_ctx:
        out_specs.append(pl.BlockSpec((n_ctx, HEAD_DIM), lambda b, h: (b, h)))
        out_shape.append(jax.ShapeDtypeStruct((n_batch * n_ctx, HALF_W), BF16))
    tot = seq + n_ctx
    outs = pl.pallas_call(
        functools.partial(_gdn_kernel, seq=seq, n_ctx=n_ctx, need_ctx=need_ctx),
        grid=(n_batch, nh),
        in_specs=in_specs,
        out_specs=out_specs,
        out_shape=out_shape,
        scratch_shapes=[pltpu.VMEM((tot, HEAD_DIM), F32), pltpu.VMEM((tot, HEAD_DIM), F32),
                        pltpu.VMEM((tot, HEAD_DIM), F32),
                        pltpu.VMEM((2, tot, HEAD_DIM), F32), pltpu.VMEM((2, tot, HEAD_DIM), F32),
                        pltpu.VMEM((2, n_chunks, GDN_CHUNK, HEAD_DIM), F32), pltpu.VMEM((2, tot, HEAD_DIM), F32),
                        pltpu.VMEM((2, n_chunks, 2 * GDN_CHUNK, HEAD_DIM), BF16),
                        pltpu.VMEM((2, n_chunks, GDN_CHUNK, GDN_CHUNK), BF16),
                        pltpu.VMEM((2, n_chunks, HEAD_DIM, GDN_CHUNK), BF16),
                        pltpu.VMEM((2, n_chunks, 1, HEAD_DIM), F32)],
        compiler_params=_cparams(("arbitrary", "arbitrary")),
        name="gdn",
    )(p, p, p, p, abh, p, p, p, p, abh, conv_w, conv_w, conv_w, a_log, dt_bias, gdn_g.reshape(1, HEAD_DIM))
    return (outs[0], outs[1]) if need_ctx else (outs[0], None)


def _softmax_pv_pair(qs, ks, v_ext):
    ss = _each(_bdot_nt, qs, ks)
    outs = []
    for s in ss:
        e = jnp.exp(s - jnp.max(s, axis=-1, keepdims=True)).astype(BF16)
        r = jnp.dot(e, v_ext, preferred_element_type=F32)
        outs.append(r[:, 0:HEAD_DIM] / r[:, HEAD_DIM:2 * HEAD_DIM])
    return outs


def _diff_kernel(*refs, seq, n_ctx, need_ctx, lambda_init):
    (q_ref, kl_ref, vl_ref, kc_ref, vc_ref, cq_ref, sq_ref, ck_ref, sk_ref,
     lq1_ref, lk1_ref, lq2_ref, lk2_ref, sg_ref) = refs[:14]
    if need_ctx:
        qc_ref, o_ref, oc_ref, k1_s, k2_s, v_s = refs[14:]
    else:
        o_ref, k1_s, k2_s, v_s = refs[14:]
    tot = seq + n_ctx
    qscale = DIFF_DH ** -0.5
    lam = (jnp.exp(jnp.sum(lq1_ref[...] * lk1_ref[...], axis=-1, keepdims=True))
           - jnp.exp(jnp.sum(lq2_ref[...] * lk2_ref[...], axis=-1, keepdims=True)) + lambda_init)

    def combine(qv, lo, hi):
        o1, o2 = _softmax_pv_pair([qv[:, 0:DIFF_DH], qv[:, DIFF_DH:2 * DIFF_DH]],
                                  [k1_s[lo:hi, :], k2_s[lo:hi, :]], v_s[lo:hi, :])
        o = o1 - lam * o2
        return (_rms(o) * sg_ref[...] * (1.0 - lambda_init)).astype(BF16)

    @pl.when(pl.program_id(2) == 0)
    def _():
        k = _rope(kl_ref[...], ck_ref[...], sk_ref[...], DIFF_DH // 4)
        k1_s[0:seq, :] = k[:, 0:DIFF_DH].astype(BF16)
        k2_s[0:seq, :] = k[:, DIFF_DH:2 * DIFF_DH].astype(BF16)
        kc = kc_ref[...]
        k1_s[seq:tot, :] = kc[:, 0:DIFF_DH].astype(BF16)
        k2_s[seq:tot, :] = kc[:, DIFF_DH:2 * DIFF_DH].astype(BF16)
        v_s[0:seq, 0:HEAD_DIM] = vl_ref[...].astype(BF16)
        v_s[seq:tot, 0:HEAD_DIM] = vc_ref[...].astype(BF16)
        v_s[:, HEAD_DIM:2 * HEAD_DIM] = jnp.ones((tot, HEAD_DIM), BF16)
        if need_ctx:
            oc_ref[...] = combine(qc_ref[...] * qscale, seq, tot)

    q = _rope(q_ref[...], cq_ref[...], sq_ref[...], DIFF_DH // 4) * qscale
    o_ref[...] = combine(q, 0, tot)


def diff_mixer(p, cos, sin, lq1, lk1, lq2, lk2, subln_g, lambda_init, n_batch, seq, n_ctx, need_ctx):
    nh = N_HEADS_HALF
    tq = 1024
    nq = seq // tq
    cb = (n_batch * seq) // n_ctx
    vec = lambda n: pl.BlockSpec((1, n), lambda b, h, i: (0, 0))
    in_specs = [
        pl.BlockSpec((tq, HEAD_DIM), lambda b, h, i: (b * nq + i, 4 * nh + h)),
        pl.BlockSpec((seq, HEAD_DIM), lambda b, h, i: (b, 5 * nh + h)),
        pl.BlockSpec((seq, HEAD_DIM), lambda b, h, i: (b, 6 * nh + h)),
        pl.BlockSpec((n_ctx, HEAD_DIM), lambda b, h, i: (cb + b, 5 * nh + h)),
        pl.BlockSpec((n_ctx, HEAD_DIM), lambda b, h, i: (cb + b, 6 * nh + h)),
        pl.BlockSpec((tq, HEAD_DIM), lambda b, h, i: (i, 0)),
        pl.BlockSpec((tq, HEAD_DIM), lambda b, h, i: (i, 0)),
        pl.BlockSpec((seq, HEAD_DIM), lambda b, h, i: (0, 0)),
        pl.BlockSpec((seq, HEAD_DIM), lambda b, h, i: (0, 0)),
        vec(DIFF_DH), vec(DIFF_DH), vec(DIFF_DH), vec(DIFF_DH), vec(HEAD_DIM),
    ]
    args = [p, p, p, p, p, cos, sin, cos, sin,
            lq1.reshape(1, -1), lk1.reshape(1, -1), lq2.reshape(1, -1), lk2.reshape(1, -1), subln_g.reshape(1, -1)]
    out_specs = [pl.BlockSpec((tq, HEAD_DIM), lambda b, h, i: (b * nq + i, h))]
    out_shape = [jax.ShapeDtypeStruct((n_batch * seq, HALF_W), BF16)]
    if need_ctx:
        in_specs.append(pl.BlockSpec((n_ctx, HEAD_DIM), lambda b, h, i: (cb + b, 4 * nh + h)))
        args.append(p)
        out_specs.append(pl.BlockSpec((n_ctx, HEAD_DIM), lambda b, h, i: (b, h)))
        out_shape.append(jax.ShapeDtypeStruct((n_batch * n_ctx, HALF_W), BF16))
    tot = seq + n_ctx
    outs = pl.pallas_call(
        functools.partial(_diff_kernel, seq=seq, n_ctx=n_ctx, need_ctx=need_ctx, lambda_init=lambda_init),
        grid=(n_batch, nh, nq),
        in_specs=in_specs,
        out_specs=out_specs,
        out_shape=out_shape,
        scratch_shapes=[pltpu.VMEM((tot, DIFF_DH), BF16), pltpu.VMEM((tot, DIFF_DH), BF16),
                        pltpu.VMEM((tot, 2 * HEAD_DIM), BF16)],
        compiler_params=_cparams(("arbitrary", "arbitrary", "arbitrary")),
        name="diff_attn",
    )(*args)
    return (outs[0], outs[1]) if need_ctx else (outs[0], None)


def _na_bias_table(rpb, n_rows):
    kr = min(NA_ROWS, n_rows)
    c = np.arange(GRID_W)[:, None]
    kc = np.arange(GRID_W)[None, :]
    win_c = np.clip(c - NA_COLS // 2, 0, GRID_W - NA_COLS)
    ok = (kc >= win_c) & (kc < win_c + NA_COLS)
    dc = np.clip(kc - c + NA_COLS - 1, 0, 2 * NA_COLS - 2)
    onehot = (dc[None] == np.arange(2 * NA_COLS - 1)[:, None, None]).astype(np.float32)
    e = jnp.einsum('hdr,rck->hdck', rpb, jnp.asarray(onehot), precision=lax.Precision.HIGHEST)
    e = jnp.where(jnp.asarray(ok)[None, None], e, NEG_BIG)
    tab = jnp.stack([e[:, NA_ROWS - 1 - o:NA_ROWS - 1 - o + kr] for o in range(NA_ROWS)], axis=1)
    return tab.transpose(0, 1, 3, 2, 4).reshape(rpb.shape[0], NA_ROWS, GRID_W, kr * GRID_W)


def _na_kernel(*refs, seq, n_ctx, need_ctx):
    q_ref, k_ref, v_ref, kc_ref, vc_ref, bias_ref = refs[:6]
    if need_ctx:
        qc_ref, o_ref, oc_ref, k_s, v_s, kc_s, vc_s = refs[6:]
    else:
        o_ref, k_s, v_s, kc_s, vc_s = refs[6:]
    n_rows = seq // GRID_W
    kr = min(NA_ROWS, n_rows)
    nl = kr * GRID_W
    scale = HEAD_DIM ** -0.5
    k_s[...] = k_ref[...].astype(BF16)
    v_s[...] = v_ref[...].astype(BF16)
    kc_s[...] = kc_ref[...].astype(BF16)
    vc_s[...] = vc_ref[...].astype(BF16)

    def row_group(it, carry):
        rs = [it * NA_ROW_GROUP + j for j in range(NA_ROW_GROUP)]
        q0s = [pl.multiple_of(r * GRID_W, GRID_W) for r in rs]
        starts = [jnp.clip(r - kr // 2, 0, n_rows - kr) for r in rs]
        k0s = [pl.multiple_of(st * GRID_W, GRID_W) for st in starts]
        qs = [q_ref[pl.ds(q0, GRID_W), :] for q0 in q0s]
        s_locs = _each(lambda q, k0, r, st: _bdot_nt(q, k_s[pl.ds(k0, nl), :]) * scale + bias_ref[r - st],
                       qs, k0s, rs, starts)
        s_ctxs = [_bdot_nt(q, kc_s[...]) * scale for q in qs]
        ms = _each(lambda a, b: jnp.maximum(jnp.max(a, axis=-1, keepdims=True), jnp.max(b, axis=-1, keepdims=True)),
                   s_locs, s_ctxs)
        e_locs = _each(lambda s, m: jnp.exp(s - m), s_locs, ms)
        e_ctxs = _each(lambda s, m: jnp.exp(s - m), s_ctxs, ms)
        dens = _each(lambda a, b: jnp.sum(a, axis=-1, keepdims=True) + jnp.sum(b, axis=-1, keepdims=True),
                     e_locs, e_ctxs)
        os_ = _each(lambda el, ec, k0: _bdot(el, v_s[pl.ds(k0, nl), :]) + _bdot(ec, vc_s[...]), e_locs, e_ctxs, k0s)
        for q0, o, den in zip(q0s, os_, dens):
            o_ref[pl.ds(q0, GRID_W), :] = (o / den).astype(BF16)
        return carry

    lax.fori_loop(0, n_rows // NA_ROW_GROUP, row_group, 0)
    if need_ctx:
        s = _bdot_nt(qc_ref[...], kc_s[...]) * scale
        e = jnp.exp(s - jnp.max(s, axis=-1, keepdims=True))
        oc_ref[...] = (_bdot(e, vc_s[...]) / jnp.sum(e, axis=-1, keepdims=True)).astype(BF16)


def na_mixer(p, bias_tab, n_batch, seq, n_ctx, need_ctx):
    nh = N_HEADS_HALF
    cb = (n_batch * seq) // n_ctx
    nl = bias_tab.shape[-1]
    in_specs = [
        pl.BlockSpec((seq, HEAD_DIM), lambda b, h: (b, h)),
        pl.BlockSpec((seq, HEAD_DIM), lambda b, h: (b, nh + h)),
        pl.BlockSpec((seq, HEAD_DIM), lambda b, h: (b, 2 * nh + h)),
        pl.BlockSpec((n_ctx, HEAD_DIM), lambda b, h: (cb + b, nh + h)),
        pl.BlockSpec((n_ctx, HEAD_DIM), lambda b, h: (cb + b, 2 * nh + h)),
        pl.BlockSpec((None, NA_ROWS, GRID_W, nl), lambda b, h: (h, 0, 0, 0)),
    ]
    args = [p, p, p, p, p, bias_tab]
    out_specs = [pl.BlockSpec((seq, HEAD_DIM), lambda b, h: (b, h))]
    out_shape = [jax.ShapeDtypeStruct((n_batch * seq, HALF_W), BF16)]
    if need_ctx:
        in_specs.append(pl.BlockSpec((n_ctx, HEAD_DIM), lambda b, h: (cb + b, h)))
        args.append(p)
        out_specs.append(pl.BlockSpec((n_ctx, HEAD_DIM), lambda b, h: (b, h)))
        out_shape.append(jax.ShapeDtypeStruct((n_batch * n_ctx, HALF_W), BF16))
    outs = pl.pallas_call(
        functools.partial(_na_kernel, seq=seq, n_ctx=n_ctx, need_ctx=need_ctx),
        grid=(n_batch, nh),
        in_specs=in_specs,
        out_specs=out_specs,
        out_shape=out_shape,
        scratch_shapes=[pltpu.VMEM((seq, HEAD_DIM), BF16), pltpu.VMEM((seq, HEAD_DIM), BF16),
                        pltpu.VMEM((n_ctx, HEAD_DIM), BF16), pltpu.VMEM((n_ctx, HEAD_DIM), BF16)],
        compiler_params=_cparams(("arbitrary", "arbitrary")),
        name="na_attn",
    )(*args)
    return (outs[0], outs[1]) if need_ctx else (outs[0], None)


def _win_kernel(*refs, seq, n_ctx, need_ctx):
    q_ref, k_ref, v_ref, kc_ref, vc_ref, cos_ref, sin_ref, sink_ref = refs[:8]
    if need_ctx:
        qc_ref, o_ref, oc_ref, k_s, v_s, kc_s, vc_s = refs[8:]
    else:
        o_ref, k_s, v_s, kc_s, vc_s = refs[8:]
    kvh = pl.program_id(1)
    wb = WIN_BLOCK
    band = 3 * wb
    scale = HEAD_DIM ** -0.5
    k_s[...] = _rope(k_ref[...], cos_ref[...], sin_ref[...], HEAD_DIM // 4).astype(BF16)
    v_s[...] = v_ref[...].astype(BF16)
    kc_s[...] = kc_ref[...].astype(BF16)
    vc_s[...] = vc_ref[...].astype(BF16)
    sinks = [jnp.full((1, 1), sink_ref[kvh * WIN_GROUP + j], F32) for j in range(WIN_GROUP)]

    def attend(qs, k_loc, v_loc, valid):
        rowmax = lambda s: jnp.max(s, axis=-1, keepdims=True)
        rowsum = lambda e: jnp.sum(e, axis=-1, keepdims=True)
        s_ctxs = [_bdot_nt(q, kc_s[...]) * scale for q in qs]
        ms = _each(lambda s, sk: jnp.maximum(rowmax(s), sk), s_ctxs, sinks)
        if k_loc is not None:
            s_locs = [jnp.where(valid, _bdot_nt(q, k_loc) * scale, NEG_BIG) for q in qs]
            ms = _each(lambda m, s: jnp.maximum(m, rowmax(s)), ms, s_locs)
        e_ctxs = _each(lambda s, m: jnp.exp(s - m), s_ctxs, ms)
        dens = _each(lambda e, sk, m: rowsum(e) + jnp.exp(sk - m), e_ctxs, sinks, ms)
        os_ = [_bdot(e, vc_s[...]) for e in e_ctxs]
        if k_loc is not None:
            e_locs = _each(lambda s, m: jnp.exp(s - m), s_locs, ms)
            dens = _each(lambda den, e: den + rowsum(e), dens, e_locs)
            os_ = _each(lambda o, e: o + _bdot(e, v_loc), os_, e_locs)
        return _each(lambda o, den: (o / den).astype(BF16), os_, dens)

    def q_block(n, carry):
        q0 = pl.multiple_of(n * wb, wb)
        k0 = pl.multiple_of(jnp.clip(n * wb - wb, 0, seq - band), wb)
        qpos = q0 + lax.broadcasted_iota(jnp.int32, (wb, band), 0)
        kpos = k0 + lax.broadcasted_iota(jnp.int32, (wb, band), 1)
        valid = jnp.abs(qpos - kpos) <= WIN
        k_loc = k_s[pl.ds(k0, band), :]
        v_loc = v_s[pl.ds(k0, band), :]
        cos = cos_ref[pl.ds(q0, wb), :]
        sin = sin_ref[pl.ds(q0, wb), :]
        qs = [_rope(q_ref[pl.ds(q0, wb), j * HEAD_DIM:(j + 1) * HEAD_DIM], cos, sin, HEAD_DIM // 4)
              for j in range(WIN_GROUP)]
        for j, o in enumerate(attend(qs, k_loc, v_loc, valid)):
            o_ref[pl.ds(q0, wb), j * HEAD_DIM:(j + 1) * HEAD_DIM] = o
        return carry

    lax.fori_loop(0, seq // wb, q_block, 0)
    if need_ctx:
        qs = [qc_ref[:, j * HEAD_DIM:(j + 1) * HEAD_DIM] for j in range(WIN_GROUP)]
        for j, o in enumerate(attend(qs, None, None, None)):
            oc_ref[:, j * HEAD_DIM:(j + 1) * HEAD_DIM] = o


def win_mixer(p, cos, sin, sink, n_batch, seq, n_ctx, need_ctx):
    nh = N_HEADS_HALF
    cb = (n_batch * seq) // n_ctx
    gw = WIN_GROUP * HEAD_DIM
    q_blk0 = (3 * nh * HEAD_DIM) // gw
    in_specs = [
        pl.BlockSpec((seq, gw), lambda b, g: (b, q_blk0 + g)),
        pl.BlockSpec((seq, HEAD_DIM), lambda b, g: (b, 4 * nh + g)),
        pl.BlockSpec((seq, HEAD_DIM), lambda b, g: (b, 4 * nh + WIN_KV_HEADS + g)),
        pl.BlockSpec((n_ctx, HEAD_DIM), lambda b, g: (cb + b, 4 * nh + g)),
        pl.BlockSpec((n_ctx, HEAD_DIM), lambda b, g: (cb + b, 4 * nh + WIN_KV_HEADS + g)),
        pl.BlockSpec((seq, HEAD_DIM), lambda b, g: (0, 0)),
        pl.BlockSpec((seq, HEAD_DIM), lambda b, g: (0, 0)),
        pl.BlockSpec(memory_space=pltpu.SMEM),
    ]
    args = [p, p, p, p, p, cos, sin, sink]
    out_specs = [pl.BlockSpec((seq, gw), lambda b, g: (b, g))]
    out_shape = [jax.ShapeDtypeStruct((n_batch * seq, HALF_W), BF16)]
    if need_ctx:
        in_specs.append(pl.BlockSpec((n_ctx, gw), lambda b, g: (cb + b, q_blk0 + g)))
        args.append(p)
        out_specs.append(pl.BlockSpec((n_ctx, gw), lambda b, g: (b, g)))
        out_shape.append(jax.ShapeDtypeStruct((n_batch * n_ctx, HALF_W), BF16))
    outs = pl.pallas_call(
        functools.partial(_win_kernel, seq=seq, n_ctx=n_ctx, need_ctx=need_ctx),
        grid=(n_batch, WIN_KV_HEADS),
        in_specs=in_specs,
        out_specs=out_specs,
        out_shape=out_shape,
        scratch_shapes=[pltpu.VMEM((seq, HEAD_DIM), BF16), pltpu.VMEM((seq, HEAD_DIM), BF16),
                        pltpu.VMEM((n_ctx, HEAD_DIM), BF16), pltpu.VMEM((n_ctx, HEAD_DIM), BF16)],
        compiler_params=_cparams(("arbitrary", "arbitrary")),
        name="win_attn",
    )(*args)
    return (outs[0], outs[1]) if need_ctx else (outs[0], None)


def kernel(x, c, ctx, c_ctx, ada_w, ada_b, norm_mix_g, norm_ffn_g, w_in_even, gdn_conv_w, gdn_a_log, gdn_dt_bias,
           gdn_norm_g, diff_lambda_q1, diff_lambda_k1, diff_lambda_q2, diff_lambda_k2, diff_subln_g, w_in_odd,
           na_rpb, win_sink, w_out, ffn_w_gate, ffn_w_up, ffn_w_down, final_norm_g):
    n_batch, seq, d = x.shape
    n_ctx = ctx.shape[1]
    n_lat = n_batch * seq
    assert n_batch + 1 <= MOD_ROWS and seq % GRID_W == 0
    hf = jnp.concatenate([x.reshape(n_lat, d), ctx.reshape(n_batch * n_ctx, d)], axis=0)
    cc = jnp.zeros((MOD_ROWS, d), F32).at[:n_batch].set(c).at[n_batch].set(c_ctx)
    mods = ada_mod(cc, ada_w, ada_b)

    cos64, sin64 = _rope_tables(seq, DIFF_DH)
    cos128, sin128 = _rope_tables(seq, HEAD_DIM)

    qkvz = 4 * HALF_W
    for l in range(DEPTH):
        need_ctx = l < DEPTH - 1
        i = l // 2
        if l % 2 == 0:
            w = w_in_even[i]
            w_main = jnp.concatenate([w[:, :qkvz], w[:, qkvz + 4 * N_HEADS_HALF:]], axis=1).astype(BF16)
            w_ab = jnp.pad(w[:, qkvz:qkvz + 4 * N_HEADS_HALF], ((0, 0), (0, HEAD_DIM - 4 * N_HEADS_HALF))).astype(BF16)
            p, ab = in_proj(hf, mods[l], norm_mix_g[l], w_main, w_ab, n_lat, seq, 1024)
            lambda_init = 0.8 - 0.6 * float(np.exp(-0.3 * l))
            ya = gdn_mixer(p, ab, gdn_conv_w[i], gdn_a_log[i], gdn_dt_bias[i], gdn_norm_g[i],
                           n_batch, seq, n_ctx, need_ctx)
            yb = diff_mixer(p, cos64, sin64, diff_lambda_q1[i], diff_lambda_k1[i], diff_lambda_q2[i],
                            diff_lambda_k2[i], diff_subln_g[i], lambda_init, n_batch, seq, n_ctx, need_ctx)
        else:
            p = in_proj(hf, mods[l], norm_mix_g[l], w_in_odd[i].astype(BF16), None, n_lat, seq, 1536)
            bias_tab = _na_bias_table(na_rpb[i], seq // GRID_W)
            ya = na_mixer(p, bias_tab, n_batch, seq, n_ctx, need_ctx)
            yb = win_mixer(p, cos128, sin128, win_sink[i], n_batch, seq, n_ctx, need_ctx)
        y_ctx = (ya[1], yb[1]) if need_ctx else None
        h1, u2 = out_proj((ya[0], yb[0]), y_ctx, hf, mods[l], norm_ffn_g[l], w_out[l].astype(BF16), n_lat, seq)
        hf = ffn_split(u2, h1, mods[l], ffn_w_gate[l].astype(BF16), ffn_w_up[l].astype(BF16),
                       ffn_w_down[l].astype(BF16), n_lat, seq)
    return final_norm(hf, final_norm_g).reshape(n_batch, seq, d)
```

```python
import functools

import jax
import jax.numpy as jnp
import numpy as np
from jax import lax
from jax.experimental import pallas as pl
from jax.experimental.pallas import tpu as pltpu

F32 = jnp.float32
BF16 = jnp.bfloat16

DEPTH = 4
GRID_W = 64
HEAD_DIM = 128
N_HEADS_HALF = 8
HALF_W = N_HEADS_HALF * HEAD_DIM
GDN_CHUNK = 64
DIFF_DH = 64
NA_ROWS = 8
NA_COLS = 16
NA_ROW_GROUP = 16
WIN = 128
WIN_BLOCK = 128
WIN_KV_HEADS = 2
WIN_BLOCK_GROUP = 2
WIN_GROUP = N_HEADS_HALF // WIN_KV_HEADS
ROPE_BASE = 10000.0
NORM_EPS = 1e-6
NEG_BIG = -1e30

V7X_VMEM_LIMIT = 56 * 1024 * 1024
MOD_ROWS = 16


def _cparams(sem):
    return pltpu.CompilerParams(dimension_semantics=sem, vmem_limit_bytes=V7X_VMEM_LIMIT)


def _bdot(a, b):
    return jnp.dot(a.astype(BF16), b.astype(BF16), preferred_element_type=F32)


def _bdot_nt(a, b):
    return lax.dot_general(a.astype(BF16), b.astype(BF16), (((1,), (1,)), ((), ())),
                           preferred_element_type=F32)


def _silu(x):
    return x * jax.nn.sigmoid(x)


def _rms(x):
    return x * lax.rsqrt(jnp.mean(x * x, axis=-1, keepdims=True) + NORM_EPS)


def _row_tile(n_lat, n_ctx, cap):
    for t in (1024, 512, 256):
        if t <= cap and n_lat % t == 0 and n_ctx % t == 0:
            return t
    raise ValueError("row counts must be multiples of 256")


def _ada_kernel(c_ref, w_ref, b_ref, o_ref):
    o_ref[...] = _bdot(_silu(c_ref[...]), w_ref[...]) + b_ref[...]


def ada_mod(cc, ada_w, ada_b):
    depth, d, n6 = ada_w.shape
    tn = 1024
    out = pl.pallas_call(
        _ada_kernel,
        grid=(depth, n6 // tn),
        in_specs=[
            pl.BlockSpec((MOD_ROWS, d), lambda l, j: (0, 0)),
            pl.BlockSpec((None, d, tn), lambda l, j: (l, 0, j)),
            pl.BlockSpec((None, 1, tn), lambda l, j: (l, 0, j)),
        ],
        out_specs=pl.BlockSpec((None, MOD_ROWS, tn), lambda l, j: (l, 0, j)),
        out_shape=jax.ShapeDtypeStruct((depth, MOD_ROWS, n6), F32),
        compiler_params=_cparams(("arbitrary", "arbitrary")),
        name="ada_mod",
    )(cc, ada_w, ada_b.reshape(depth, 1, n6))
    return out.reshape(depth, MOD_ROWS, 6, d)


def _norm_mod(x, g, shift, scale):
    return (_rms(x) * g) * (1.0 + scale) + shift


def _seg_fn(n_lat, seq, tm):
    n_lat_tiles = n_lat // tm
    n_batch = n_lat // seq
    return lambda i: jnp.where(i < n_lat_tiles, (i * tm) // seq, n_batch)


def _inproj_kernel(h_ref, mod_ref, g_ref, w_ref, *rest, has_aux):
    if has_aux:
        waux_ref, o_ref, oaux_ref, u_scr = rest
    else:
        o_ref, u_scr = rest

    @pl.when(pl.program_id(1) == 0)
    def _():
        u = _norm_mod(h_ref[...], g_ref[...], mod_ref[0:1, :], mod_ref[1:2, :])
        u_scr[...] = u.astype(BF16)
        if has_aux:
            oaux_ref[...] = jnp.dot(u_scr[...], waux_ref[...], preferred_element_type=F32)

    o_ref[...] = jnp.dot(u_scr[...], w_ref[...], preferred_element_type=F32)


def in_proj(hf, mod_l, g, w, w_aux, n_lat, seq, tn):
    m, d = hf.shape
    n = w.shape[1]
    tm = _row_tile(n_lat, m - n_lat, 1024)
    seg = _seg_fn(n_lat, seq, tm)
    has_aux = w_aux is not None
    in_specs = [
        pl.BlockSpec((tm, d), lambda i, j: (i, 0)),
        pl.BlockSpec((None, 6, d), lambda i, j: (seg(i), 0, 0)),
        pl.BlockSpec((1, d), lambda i, j: (0, 0)),
        pl.BlockSpec((d, tn), lambda i, j: (0, j)),
    ]
    out_specs = [pl.BlockSpec((tm, tn), lambda i, j: (i, j))]
    out_shape = [jax.ShapeDtypeStruct((m, n), F32)]
    args = [hf, mod_l, g.reshape(1, d), w]
    if has_aux:
        in_specs.append(pl.BlockSpec((d, HEAD_DIM), lambda i, j: (0, 0)))
        out_specs.append(pl.BlockSpec((tm, HEAD_DIM), lambda i, j: (i, 0)))
        out_shape.append(jax.ShapeDtypeStruct((m, HEAD_DIM), F32))
        args.append(w_aux)
    outs = pl.pallas_call(
        functools.partial(_inproj_kernel, has_aux=has_aux),
        grid=(m // tm, n // tn),
        in_specs=in_specs,
        out_specs=out_specs,
        out_shape=out_shape,
        scratch_shapes=[pltpu.VMEM((tm, d), BF16)],
        compiler_params=_cparams(("arbitrary", "arbitrary")),
        name="in_proj",
    )(*args)
    return outs if has_aux else outs[0]


def _outproj_kernel(*refs, n_lat_tiles, has_ctx):
    if has_ctx:
        yal_ref, ybl_ref, yac_ref, ybc_ref, h_ref, mod_ref, g_ref, w_ref, ho_ref, u_ref = refs
        is_ctx = pl.program_id(0) >= n_lat_tiles
        ya = jnp.where(is_ctx, yac_ref[...], yal_ref[...])
        yb = jnp.where(is_ctx, ybc_ref[...], ybl_ref[...])
    else:
        yal_ref, ybl_ref, h_ref, mod_ref, g_ref, w_ref, ho_ref, u_ref = refs
        ya, yb = yal_ref[...], ybl_ref[...]
    y = (jnp.dot(ya, w_ref[0:HALF_W, :], preferred_element_type=F32)
         + jnp.dot(yb, w_ref[HALF_W:2 * HALF_W, :], preferred_element_type=F32))
    hn = h_ref[...] + mod_ref[2:3, :] * y
    ho_ref[...] = hn
    u_ref[...] = _norm_mod(hn, g_ref[...], mod_ref[3:4, :], mod_ref[4:5, :]).astype(BF16)


def out_proj(y_lat, y_ctx, hf, mod_l, g, w, n_lat, seq):
    m, d = hf.shape
    has_ctx = y_ctx is not None
    m_out = m if has_ctx else n_lat
    tm = _row_tile(n_lat, m - n_lat, 512)
    n_lat_tiles = n_lat // tm
    seg = _seg_fn(n_lat, seq, tm)
    lat_spec = pl.BlockSpec((tm, HALF_W), lambda i: (jnp.minimum(i, n_lat_tiles - 1), 0))
    ctx_spec = pl.BlockSpec((tm, HALF_W), lambda i: (jnp.maximum(i - n_lat_tiles, 0), 0))
    in_specs = [lat_spec, lat_spec] + ([ctx_spec, ctx_spec] if has_ctx else []) + [
        pl.BlockSpec((tm, d), lambda i: (i, 0)),
        pl.BlockSpec((None, 6, d), lambda i: (seg(i), 0, 0)),
        pl.BlockSpec((1, d), lambda i: (0, 0)),
        pl.BlockSpec((d, d), lambda i: (0, 0)),
    ]
    args = list(y_lat) + (list(y_ctx) if has_ctx else []) + [hf, mod_l, g.reshape(1, d), w]
    return pl.pallas_call(
        functools.partial(_outproj_kernel, n_lat_tiles=n_lat_tiles, has_ctx=has_ctx),
        grid=(m_out // tm,),
        in_specs=in_specs,
        out_specs=[pl.BlockSpec((tm, d), lambda i: (i, 0)), pl.BlockSpec((tm, d), lambda i: (i, 0))],
        out_shape=[jax.ShapeDtypeStruct((m_out, d), F32), jax.ShapeDtypeStruct((m_out, d), BF16)],
        compiler_params=_cparams(("arbitrary",)),
        name="out_proj",
    )(*args)


def _ffn_act_kernel(u_ref, wg_ref, wu_ref, o_ref):
    u = u_ref[...]
    gate = jnp.dot(u, wg_ref[...], preferred_element_type=F32)
    up = jnp.dot(u, wu_ref[...], preferred_element_type=F32)
    o_ref[...] = (_silu(gate) * up).astype(BF16)


def _ffn_down_kernel(a_ref, h_ref, mod_ref, wd_ref, o_ref):
    o_ref[...] = h_ref[...] + mod_ref[5:6, :] * jnp.dot(a_ref[...], wd_ref[...], preferred_element_type=F32)


def _final_norm_kernel(h_ref, g_ref, o_ref):
    o_ref[...] = _rms(h_ref[...]) * g_ref[...]


def ffn_split(u2, h1, mod_l, wg, wu, wd, n_lat, seq):
    m, d = h1.shape
    hidden = wg.shape[1]
    tm = _row_tile(n_lat, m - n_lat, 1024)
    tn = 512
    seg = _seg_fn(n_lat, seq, tm)
    act = pl.pallas_call(
        _ffn_act_kernel,
        grid=(m // tm, hidden // tn),
        in_specs=[
            pl.BlockSpec((tm, d), lambda i, j: (i, 0)),
            pl.BlockSpec((d, tn), lambda i, j: (0, j)),
            pl.BlockSpec((d, tn), lambda i, j: (0, j)),
        ],
        out_specs=pl.BlockSpec((tm, tn), lambda i, j: (i, j)),
        out_shape=jax.ShapeDtypeStruct((m, hidden), BF16),
        compiler_params=_cparams(("arbitrary", "arbitrary")),
        name="ffn_act",
    )(u2, wg, wu)
    return pl.pallas_call(
        _ffn_down_kernel,
        grid=(m // tm, d // tn),
        in_specs=[
            pl.BlockSpec((tm, hidden), lambda i, j: (i, 0)),
            pl.BlockSpec((tm, tn), lambda i, j: (i, j)),
            pl.BlockSpec((None, 6, tn), lambda i, j: (seg(i), 0, j)),
            pl.BlockSpec((hidden, tn), lambda i, j: (0, j)),
        ],
        out_specs=pl.BlockSpec((tm, tn), lambda i, j: (i, j)),
        out_shape=jax.ShapeDtypeStruct((m, d), F32),
        compiler_params=_cparams(("arbitrary", "arbitrary")),
        name="ffn_down",
    )(act, h1, mod_l, wd)


def final_norm(h, g):
    m, d = h.shape
    tm = 512
    return pl.pallas_call(
        _final_norm_kernel,
        grid=(m // tm,),
        in_specs=[pl.BlockSpec((tm, d), lambda i: (i, 0)), pl.BlockSpec((1, d), lambda i: (0, 0))],
        out_specs=pl.BlockSpec((tm, d), lambda i: (i, 0)),
        out_shape=jax.ShapeDtypeStruct((m, d), F32),
        compiler_params=_cparams(("arbitrary",)),
        name="final_norm",
    )(h, g.reshape(1, d))


def _rope_tables(seq, comp):
    pos = np.arange(seq)
    rows, cols = pos // GRID_W, pos % GRID_W
    n = comp // 2
    inv = (ROPE_BASE ** (-np.arange(0, n, 2, dtype=np.float32) / n)).astype(np.float32)
    ang_r = rows.astype(np.float32)[:, None] * inv[None, :]
    ang_c = cols.astype(np.float32)[:, None] * inv[None, :]
    ang = jnp.asarray(np.concatenate([ang_r, ang_r, ang_c, ang_c], axis=1))
    sign = np.concatenate([-np.ones(n // 2), np.ones(n // 2)] * 2).astype(np.float32)
    reps = HEAD_DIM // comp
    return jnp.tile(jnp.cos(ang), (1, reps)), jnp.tile(jnp.sin(ang) * jnp.asarray(sign)[None, :], (1, reps))


def _rope(x, cos, sin, half):
    lane = lax.broadcasted_iota(jnp.int32, x.shape, 1)
    first = (lane % (2 * half)) < half
    fwd = pltpu.roll(x, HEAD_DIM - half, 1)
    bwd = pltpu.roll(x, half, 1)
    return x * cos + jnp.where(first, fwd, bwd) * sin


def _shift_rows(x, up):
    t = x.shape[0]
    row = lax.broadcasted_iota(jnp.int32, x.shape, 0)
    if up:
        return jnp.where(row == t - 1, 0.0, pltpu.roll(x, t - 1, 0))
    return jnp.where(row == 0, 0.0, pltpu.roll(x, 1, 0))


def _conv_silu(x, w):
    y = _shift_rows(x, False) * w[0:1, :] + x * w[1:2, :] + _shift_rows(x, True) * w[2:3, :]
    return _silu(y)


def _l2n(x):
    return x * lax.rsqrt(jnp.sum(x * x, axis=-1, keepdims=True) + NORM_EPS)


def _softplus(x):
    return jnp.maximum(x, 0.0) + jnp.log1p(jnp.exp(-jnp.abs(x)))


TRI_BASE = 8
GDN_PREP_GROUPS = (6, 4, 2, 1)


def _each(fn, *lists):
    return [fn(*xs) for xs in zip(*lists)]


def _weave(*gens):
    results = [None] * len(gens)
    live = list(range(len(gens)))
    while live:
        for i in list(live):
            try:
                next(gens[i])
            except StopIteration as stop:
                results[i] = stop.value
                live.remove(i)
    return results


def _unit_tri_inverse_stages(mats):
    c = mats[0].shape[0]
    row = lax.broadcasted_iota(jnp.int32, (c, c), 0)
    col = lax.broadcasted_iota(jnp.int32, (c, c), 1)
    same = lambda n: (row // n) == (col // n)
    eye = (row == col).astype(F32)
    xs = [jnp.where(same(TRI_BASE), -a, 0.0) for a in mats]
    ts = [eye + x for x in xs]
    p = 2
    while p < TRI_BASE:
        xs = [_bdot(x, x) for x in xs]
        yield
        ts = _each(lambda t, x: t + _bdot(t, x), ts, xs)
        yield
        p *= 2
    n = 2 * TRI_BASE
    while n <= c:
        off = same(n) & jnp.logical_not(same(n // 2))
        tas = _each(lambda t, a: _bdot(t, jnp.where(off, a, 0.0)), ts, mats)
        yield
        ts = _each(lambda t, ta: t - _bdot(ta, t), ts, tas)
        yield
        n *= 2
    return ts


def _seg_cumsum(x, rev):
    t = x.shape[0]
    pos = lax.broadcasted_iota(jnp.int32, x.shape, 0) % GDN_CHUNK
    s = 1
    while s < GDN_CHUNK:
        if rev:
            x = x + jnp.where(pos < GDN_CHUNK - s, pltpu.roll(x, t - s, 0), 0.0)
        else:
            x = x + jnp.where(pos >= s, pltpu.roll(x, s, 0), 0.0)
        s *= 2
    return x


def _gdn_prep_stages(qs, ks, vs, gcs, betas, revs):
    c = qs[0].shape[0]
    row = lax.broadcasted_iota(jnp.int32, (c, c), 0)
    col = lax.broadcasted_iota(jnp.int32, (c, c), 1)
    tri = {False: row >= col, True: row <= col}
    strict = {False: row > col, True: row < col}
    decays = _each(lambda gc, rev: jnp.exp(jnp.where(tri[rev], gc[:, 0:c] - jnp.transpose(gc)[0:c, :], -jnp.inf)),
                   gcs, revs)
    kbs = _each(lambda k, b: k * b, ks, betas)
    kks = _each(_bdot_nt, kbs, ks)
    qsc = [q * (HEAD_DIM ** -0.5) for q in qs]
    qks = _each(_bdot_nt, qsc, ks)
    yield
    a_mats = _each(lambda kk, dec, rev: jnp.where(strict[rev], kk * dec, 0.0), kks, decays, revs)
    t_mats = yield from _unit_tri_inverse_stages(a_mats)
    egs = [jnp.exp(gc) for gc in gcs]
    rhs = _each(lambda v, b, kb, eg: jnp.concatenate([v * b, kb * eg], axis=1), vs, betas, kbs, egs)
    uws = _each(_bdot, t_mats, rhs)
    yield
    attns = _each(lambda qk, dec, rev: jnp.where(tri[rev], qk * dec, 0.0).astype(BF16), qks, decays, revs)
    g_lasts = _each(lambda gc, rev: gc[0:1, :] if rev else gc[c - 1:c, :], gcs, revs)
    kdts = _each(lambda k, gl, gc: jnp.transpose(k * jnp.exp(gl - gc)).astype(BF16), ks, g_lasts, gcs)
    return [(uw[:, 0:HEAD_DIM], uw[:, HEAD_DIM:].astype(BF16), (q * eg).astype(BF16), attn, kdt, jnp.exp(gl))
            for uw, q, eg, attn, kdt, gl in zip(uws, qsc, egs, attns, kdts, g_lasts)]


def _gdn_kernel(*refs, seq, n_ctx, need_ctx):
    (ql_ref, kl_ref, vl_ref, zl_ref, abl_ref, qc_ref, kc_ref, vc_ref, zc_ref, abc_ref,
     cwq_ref, cwk_ref, cwv_ref, alog_ref, dtb_ref, gn_ref) = refs[:16]
    if need_ctx:
        ol_ref, oc_ref = refs[16:18]
        scr = refs[18:]
    else:
        ol_ref = refs[16]
        scr = refs[17:]
    q_s, k_s, v_s, gc_s, bt_s, u_s, o_s, wq_s, at_s, kd_s, gl_s = scr
    hd = pl.program_id(1)
    c = GDN_CHUNK
    for (src_q, src_k, src_v, lo, n) in ((qc_ref, kc_ref, vc_ref, 0, n_ctx), (ql_ref, kl_ref, vl_ref, n_ctx, seq)):
        q_s[lo:lo + n, :] = _l2n(_conv_silu(src_q[...], cwq_ref[...]))
        k_s[lo:lo + n, :] = _l2n(_conv_silu(src_k[...], cwk_ref[...]))
        v_s[lo:lo + n, :] = _conv_silu(src_v[...], cwv_ref[...])
    lane4 = lax.broadcasted_iota(jnp.int32, (1, 4), 1)
    neg_decay = -jnp.exp(jnp.where(lane4 == 0, alog_ref[0, hd], alog_ref[1, hd]))
    dt_b = jnp.where(lane4 == 0, dtb_ref[0, hd], dtb_ref[1, hd])
    for (src, lo, n) in ((abc_ref, 0, n_ctx), (abl_ref, n_ctx, seq)):
        ab = src[...]
        g = neg_decay * _softplus(ab + dt_b)
        beta = jax.nn.sigmoid(ab)
        for d in range(2):
            gc_s[d, lo:lo + n, :] = jnp.broadcast_to(_seg_cumsum(g, d == 1)[:, d:d + 1], (n, HEAD_DIM))
            bt_s[d, lo:lo + n, :] = jnp.broadcast_to(beta[:, 2 + d:3 + d], (n, HEAD_DIM))

    n_chunks = (n_ctx + seq) // c
    n_cc = n_ctx // c

    group = next(g for g in GDN_PREP_GROUPS if n_chunks % g == 0)
    n_groups = n_chunks // group

    def chunk_of(d, t):
        return t if d == 0 else jnp.where(t < n_cc, n_cc - 1 - t, n_chunks - 1 + n_cc - t)

    def prep_group(g):
        keys = [(d, g * group + j) for j in range(group) for d in range(2)]
        r0s = [pl.multiple_of(chunk_of(d, t) * c, c) for d, t in keys]
        res = yield from _gdn_prep_stages(
            [q_s[pl.ds(r0, c), :] for r0 in r0s], [k_s[pl.ds(r0, c), :] for r0 in r0s],
            [v_s[pl.ds(r0, c), :] for r0 in r0s], [gc_s[d, pl.ds(r0, c), :] for (d, _), r0 in zip(keys, r0s)],
            [bt_s[d, pl.ds(r0, c), :] for (d, _), r0 in zip(keys, r0s)], [d == 1 for d, _ in keys])
        for (d, t), (u, w, qg, attn, kdt, egl) in zip(keys, res):
            u_s[d, t] = u
            wq_s[d, t, 0:c, :] = w
            wq_s[d, t, c:2 * c, :] = qg
            at_s[d, t] = attn
            kd_s[d, t] = kdt
            gl_s[d, t] = egl

    def scan_group(g, carry):
        for j in range(group):
            t = g * group + j
            r0s = [pl.multiple_of(chunk_of(d, t) * c, c) for d in range(2)]
            wss = [jnp.dot(wq_s[d, t], carry[d].astype(BF16), preferred_element_type=F32) for d in range(2)]
            yield
            v_news = [(u_s[d, t] - wss[d][0:c, :]).astype(BF16) for d in range(2)]
            nxt = tuple(carry[d] * gl_s[d, t] + jnp.dot(kd_s[d, t], v_news[d], preferred_element_type=F32)
                        for d in range(2))
            for d in range(2):
                o_s[d, pl.ds(r0s[d], c), :] = (wss[d][c:2 * c, :]
                                               + jnp.dot(at_s[d, t], v_news[d], preferred_element_type=F32))
            yield
            carry = nxt
        return carry

    def pipe_body(g, carry):
        carry, _ = _weave(scan_group(g - 1, carry), prep_group(g))
        return carry

    zero = jnp.zeros((HEAD_DIM, HEAD_DIM), F32)
    _weave(prep_group(0))
    carry = lax.fori_loop(1, n_groups, pipe_body, (zero, zero))
    _weave(scan_group(n_groups - 1, carry))

    def finish(lo, n, z_ref, out_ref):
        o = o_s[0, lo:lo + n, :] + o_s[1, lo:lo + n, :]
        out_ref[...] = (_rms(o) * gn_ref[...] * _silu(z_ref[...])).astype(out_ref.dtype)

    finish(n_ctx, seq, zl_ref, ol_ref)
    if need_ctx:
        finish(0, n_ctx, zc_ref, oc_ref)


def gdn_mixer(p, ab, conv_w, a_log, dt_bias, gdn_g, n_batch, seq, n_ctx, need_ctx):
    cb = (n_batch * seq) // n_ctx
    nh = N_HEADS_HALF
    m = p.shape[0]
    abh = ab[:, 0:4 * nh].reshape(m, 4, nh).transpose(2, 0, 1)
    n_chunks = (seq + n_ctx) // GDN_CHUNK

    def lat(col0):
        return pl.BlockSpec((seq, HEAD_DIM), lambda b, h: (b, col0 + h))

    def ctxs(col0):
        return pl.BlockSpec((n_ctx, HEAD_DIM), lambda b, h: (cb + b, col0 + h))

    in_specs = [lat(0), lat(nh), lat(2 * nh), lat(3 * nh),
                pl.BlockSpec((None, seq, 4), lambda b, h: (h, b, 0)),
                ctxs(0), ctxs(nh), ctxs(2 * nh), ctxs(3 * nh),
                pl.BlockSpec((None, n_ctx, 4), lambda b, h: (h, cb + b, 0)),
                pl.BlockSpec((3, HEAD_DIM), lambda b, h: (0, h)),
                pl.BlockSpec((3, HEAD_DIM), lambda b, h: (0, nh + h)),
                pl.BlockSpec((3, HEAD_DIM), lambda b, h: (0, 2 * nh + h)),
                pl.BlockSpec(memory_space=pltpu.SMEM),
                pl.BlockSpec(memory_space=pltpu.SMEM),
                pl.BlockSpec((1, HEAD_DIM), lambda b, h: (0, 0))]
    out_specs = [pl.BlockSpec((seq, HEAD_DIM), lambda b, h: (b, h))]
    out_shape = [jax.ShapeDtypeStruct((n_batch * seq, HALF_W), BF16)]
    if need_ctx:
        out_specs.append(pl.BlockSpec((n_ctx, HEAD_DIM), lambda b, h: (b, h)))
        out_shape.append(jax.ShapeDtypeStruct((n_batch * n_ctx, HALF_W), BF16))
    tot = seq + n_ctx
    outs = pl.pallas_call(
        functools.partial(_gdn_kernel, seq=seq, n_ctx=n_ctx, need_ctx=need_ctx),
        grid=(n_batch, nh),
        in_specs=in_specs,
        out_specs=out_specs,
        out_shape=out_shape,
        scratch_shapes=[pltpu.VMEM((tot, HEAD_DIM), F32), pltpu.VMEM((tot, HEAD_DIM), F32),
                        pltpu.VMEM((tot, HEAD_DIM), F32),
                        pltpu.VMEM((2, tot, HEAD_DIM), F32), pltpu.VMEM((2, tot, HEAD_DIM), F32),
                        pltpu.VMEM((2, n_chunks, GDN_CHUNK, HEAD_DIM), F32), pltpu.VMEM((2, tot, HEAD_DIM), F32),
                        pltpu.VMEM((2, n_chunks, 2 * GDN_CHUNK, HEAD_DIM), BF16),
                        pltpu.VMEM((2, n_chunks, GDN_CHUNK, GDN_CHUNK), BF16),
                        pltpu.VMEM((2, n_chunks, HEAD_DIM, GDN_CHUNK), BF16),
                        pltpu.VMEM((2, n_chunks, 1, HEAD_DIM), F32)],
        compiler_params=_cparams(("arbitrary", "arbitrary")),
        name="gdn",
    )(p, p, p, p, abh, p, p, p, p, abh, conv_w, conv_w, conv_w, a_log, dt_bias, gdn_g.reshape(1, HEAD_DIM))
    return (outs[0], outs[1]) if need_ctx else (outs[0], None)


def _softmax_pv_pair(qs, ks, v_ext):
    ss = _each(_bdot_nt, qs, ks)
    outs = []
    for s in ss:
        e = jnp.exp(s - jnp.max(s, axis=-1, keepdims=True)).astype(BF16)
        r = jnp.dot(e, v_ext, preferred_element_type=F32)
        outs.append(r[:, 0:HEAD_DIM] / r[:, HEAD_DIM:2 * HEAD_DIM])
    return outs


def _diff_kernel(*refs, seq, n_ctx, need_ctx, lambda_init):
    (q_ref, kl_ref, vl_ref, kc_ref, vc_ref, cq_ref, sq_ref, ck_ref, sk_ref,
     lq1_ref, lk1_ref, lq2_ref, lk2_ref, sg_ref) = refs[:14]
    if need_ctx:
        qc_ref, o_ref, oc_ref, k1_s, k2_s, v_s = refs[14:]
    else:
        o_ref, k1_s, k2_s, v_s = refs[14:]
    tot = seq + n_ctx
    qscale = DIFF_DH ** -0.5
    lam = (jnp.exp(jnp.sum(lq1_ref[...] * lk1_ref[...], axis=-1, keepdims=True))
           - jnp.exp(jnp.sum(lq2_ref[...] * lk2_ref[...], axis=-1, keepdims=True)) + lambda_init)

    def combine(qv, lo, hi):
        o1, o2 = _softmax_pv_pair([qv[:, 0:DIFF_DH], qv[:, DIFF_DH:2 * DIFF_DH]],
                                  [k1_s[lo:hi, :], k2_s[lo:hi, :]], v_s[lo:hi, :])
        o = o1 - lam * o2
        return (_rms(o) * sg_ref[...] * (1.0 - lambda_init)).astype(BF16)

    @pl.when(pl.program_id(2) == 0)
    def _():
        k = _rope(kl_ref[...], ck_ref[...], sk_ref[...], DIFF_DH // 4)
        k1_s[0:seq, :] = k[:, 0:DIFF_DH].astype(BF16)
        k2_s[0:seq, :] = k[:, DIFF_DH:2 * DIFF_DH].astype(BF16)
        kc = kc_ref[...]
        k1_s[seq:tot, :] = kc[:, 0:DIFF_DH].astype(BF16)
        k2_s[seq:tot, :] = kc[:, DIFF_DH:2 * DIFF_DH].astype(BF16)
        v_s[0:seq, 0:HEAD_DIM] = vl_ref[...].astype(BF16)
        v_s[seq:tot, 0:HEAD_DIM] = vc_ref[...].astype(BF16)
        v_s[:, HEAD_DIM:2 * HEAD_DIM] = jnp.ones((tot, HEAD_DIM), BF16)
        if need_ctx:
            oc_ref[...] = combine(qc_ref[...] * qscale, seq, tot)

    q = _rope(q_ref[...], cq_ref[...], sq_ref[...], DIFF_DH // 4) * qscale
    o_ref[...] = combine(q, 0, tot)


def diff_mixer(p, cos, sin, lq1, lk1, lq2, lk2, subln_g, lambda_init, n_batch, seq, n_ctx, need_ctx):
    nh = N_HEADS_HALF
    tq = 1024
    nq = seq // tq
    cb = (n_batch * seq) // n_ctx
    vec = lambda n: pl.BlockSpec((1, n), lambda b, h, i: (0, 0))
    in_specs = [
        pl.BlockSpec((tq, HEAD_DIM), lambda b, h, i: (b * nq + i, 4 * nh + h)),
        pl.BlockSpec((seq, HEAD_DIM), lambda b, h, i: (b, 5 * nh + h)),
        pl.BlockSpec((seq, HEAD_DIM), lambda b, h, i: (b, 6 * nh + h)),
        pl.BlockSpec((n_ctx, HEAD_DIM), lambda b, h, i: (cb + b, 5 * nh + h)),
        pl.BlockSpec((n_ctx, HEAD_DIM), lambda b, h, i: (cb + b, 6 * nh + h)),
        pl.BlockSpec((tq, HEAD_DIM), lambda b, h, i: (i, 0)),
        pl.BlockSpec((tq, HEAD_DIM), lambda b, h, i: (i, 0)),
        pl.BlockSpec((seq, HEAD_DIM), lambda b, h, i: (0, 0)),
        pl.BlockSpec((seq, HEAD_DIM), lambda b, h, i: (0, 0)),
        vec(DIFF_DH), vec(DIFF_DH), vec(DIFF_DH), vec(DIFF_DH), vec(HEAD_DIM),
    ]
    args = [p, p, p, p, p, cos, sin, cos, sin,
            lq1.reshape(1, -1), lk1.reshape(1, -1), lq2.reshape(1, -1), lk2.reshape(1, -1), subln_g.reshape(1, -1)]
    out_specs = [pl.BlockSpec((tq, HEAD_DIM), lambda b, h, i: (b * nq + i, h))]
    out_shape = [jax.ShapeDtypeStruct((n_batch * seq, HALF_W), BF16)]
    if need_ctx:
        in_specs.append(pl.BlockSpec((n_ctx, HEAD_DIM), lambda b, h, i: (cb + b, 4 * nh + h)))
        args.append(p)
        out_specs.append(pl.BlockSpec((n_ctx, HEAD_DIM), lambda b, h, i: (b, h)))
        out_shape.append(jax.ShapeDtypeStruct((n_batch * n_ctx, HALF_W), BF16))
    tot = seq + n_ctx
    outs = pl.pallas_call(
        functools.partial(_diff_kernel, seq=seq, n_ctx=n_ctx, need_ctx=need_ctx, lambda_init=lambda_init),
        grid=(n_batch, nh, nq),
        in_specs=in_specs,
        out_specs=out_specs,
        out_shape=out_shape,
        scratch_shapes=[pltpu.VMEM((tot, DIFF_DH), BF16), pltpu.VMEM((tot, DIFF_DH), BF16),
                        pltpu.VMEM((tot, 2 * HEAD_DIM), BF16)],
        compiler_params=_cparams(("arbitrary", "arbitrary", "arbitrary")),
        name="diff_attn",
    )(*args)
    return (outs[0], outs[1]) if need_ctx else (outs[0], None)


def _na_bias_table(rpb, n_rows):
    kr = min(NA_ROWS, n_rows)
    c = np.arange(GRID_W)[:, None]
    kc = np.arange(GRID_W)[None, :]
    win_c = np.clip(c - NA_COLS // 2, 0, GRID_W - NA_COLS)
    ok = (kc >= win_c) & (kc < win_c + NA_COLS)
    dc = np.clip(kc - c + NA_COLS - 1, 0, 2 * NA_COLS - 2)
    onehot = (dc[None] == np.arange(2 * NA_COLS - 1)[:, None, None]).astype(np.float32)
    e = jnp.einsum('hdr,rck->hdck', rpb, jnp.asarray(onehot), precision=lax.Precision.HIGHEST)
    e = jnp.where(jnp.asarray(ok)[None, None], e, NEG_BIG)
    tab = jnp.stack([e[:, NA_ROWS - 1 - o:NA_ROWS - 1 - o + kr] for o in range(NA_ROWS)], axis=1)
    return tab.transpose(0, 1, 3, 2, 4).reshape(rpb.shape[0], NA_ROWS, GRID_W, kr * GRID_W)


def _na_kernel(*refs, seq, n_ctx, need_ctx):
    q_ref, k_ref, v_ref, kc_ref, vc_ref, bias_ref = refs[:6]
    if need_ctx:
        qc_ref, o_ref, oc_ref, k_s, v_s, kc_s, vc_s = refs[6:]
    else:
        o_ref, k_s, v_s, kc_s, vc_s = refs[6:]
    n_rows = seq // GRID_W
    kr = min(NA_ROWS, n_rows)
    nl = kr * GRID_W
    scale = HEAD_DIM ** -0.5
    k_s[...] = k_ref[...].astype(BF16)
    v_s[...] = v_ref[...].astype(BF16)
    kc_s[...] = kc_ref[...].astype(BF16)
    vc_s[...] = vc_ref[...].astype(BF16)

    def row_group(it, carry):
        rs = [it * NA_ROW_GROUP + j for j in range(NA_ROW_GROUP)]
        q0s = [pl.multiple_of(r * GRID_W, GRID_W) for r in rs]
        starts = [jnp.clip(r - kr // 2, 0, n_rows - kr) for r in rs]
        k0s = [pl.multiple_of(st * GRID_W, GRID_W) for st in starts]
        qs = [q_ref[pl.ds(q0, GRID_W), :] for q0 in q0s]
        s_locs = _each(lambda q, k0, r, st: _bdot_nt(q, k_s[pl.ds(k0, nl), :]) * scale + bias_ref[r - st],
                       qs, k0s, rs, starts)
        s_ctxs = [_bdot_nt(q, kc_s[...]) * scale for q in qs]
        ms = _each(lambda a, b: jnp.maximum(jnp.max(a, axis=-1, keepdims=True), jnp.max(b, axis=-1, keepdims=True)),
                   s_locs, s_ctxs)
        e_locs = _each(lambda s, m: jnp.exp(s - m), s_locs, ms)
        e_ctxs = _each(lambda s, m: jnp.exp(s - m), s_ctxs, ms)
        dens = _each(lambda a, b: jnp.sum(a, axis=-1, keepdims=True) + jnp.sum(b, axis=-1, keepdims=True),
                     e_locs, e_ctxs)
        os_ = _each(lambda el, ec, k0: _bdot(el, v_s[pl.ds(k0, nl), :]) + _bdot(ec, vc_s[...]), e_locs, e_ctxs, k0s)
        for q0, o, den in zip(q0s, os_, dens):
            o_ref[pl.ds(q0, GRID_W), :] = (o / den).astype(BF16)
        return carry

    lax.fori_loop(0, n_rows // NA_ROW_GROUP, row_group, 0)
    if need_ctx:
        s = _bdot_nt(qc_ref[...], kc_s[...]) * scale
        e = jnp.exp(s - jnp.max(s, axis=-1, keepdims=True))
        oc_ref[...] = (_bdot(e, vc_s[...]) / jnp.sum(e, axis=-1, keepdims=True)).astype(BF16)


def na_mixer(p, bias_tab, n_batch, seq, n_ctx, need_ctx):
    nh = N_HEADS_HALF
    cb = (n_batch * seq) // n_ctx
    nl = bias_tab.shape[-1]
    in_specs = [
        pl.BlockSpec((seq, HEAD_DIM), lambda b, h: (b, h)),
        pl.BlockSpec((seq, HEAD_DIM), lambda b, h: (b, nh + h)),
        pl.BlockSpec((seq, HEAD_DIM), lambda b, h: (b, 2 * nh + h)),
        pl.BlockSpec((n_ctx, HEAD_DIM), lambda b, h: (cb + b, nh + h)),
        pl.BlockSpec((n_ctx, HEAD_DIM), lambda b, h: (cb + b, 2 * nh + h)),
        pl.BlockSpec((None, NA_ROWS, GRID_W, nl), lambda b, h: (h, 0, 0, 0)),
    ]
    args = [p, p, p, p, p, bias_tab]
    out_specs = [pl.BlockSpec((seq, HEAD_DIM), lambda b, h: (b, h))]
    out_shape = [jax.ShapeDtypeStruct((n_batch * seq, HALF_W), BF16)]
    if need_ctx:
        in_specs.append(pl.BlockSpec((n_ctx, HEAD_DIM), lambda b, h: (cb + b, h)))
        args.append(p)
        out_specs.append(pl.BlockSpec((n_ctx, HEAD_DIM), lambda b, h: (b, h)))
        out_shape.append(jax.ShapeDtypeStruct((n_batch * n_ctx, HALF_W), BF16))
    outs = pl.pallas_call(
        functools.partial(_na_kernel, seq=seq, n_ctx=n_ctx, need_ctx=need_ctx),
        grid=(n_batch, nh),
        in_specs=in_specs,
        out_specs=out_specs,
        out_shape=out_shape,
        scratch_shapes=[pltpu.VMEM((seq, HEAD_DIM), BF16), pltpu.VMEM((seq, HEAD_DIM), BF16),
                        pltpu.VMEM((n_ctx, HEAD_DIM), BF16), pltpu.VMEM((n_ctx, HEAD_DIM), BF16)],
        compiler_params=_cparams(("arbitrary", "arbitrary")),
        name="na_attn",
    )(*args)
    return (outs[0], outs[1]) if need_ctx else (outs[0], None)


def _win_kernel(*refs, seq, n_ctx, need_ctx):
    q_ref, k_ref, v_ref, kc_ref, vc_ref, cos_ref, sin_ref, sink_ref = refs[:8]
    if need_ctx:
        qc_ref, o_ref, oc_ref, k_s, v_s, kc_s, vc_s = refs[8:]
    else:
        o_ref, k_s, v_s, kc_s, vc_s = refs[8:]
    kvh = pl.program_id(1)
    wb = WIN_BLOCK
    band = 3 * wb
    scale = HEAD_DIM ** -0.5
    k_s[...] = _rope(k_ref[...], cos_ref[...], sin_ref[...], HEAD_DIM // 4).astype(BF16)
    v_s[...] = v_ref[...].astype(BF16)
    kc_s[...] = kc_ref[...].astype(BF16)
    vc_s[...] = vc_ref[...].astype(BF16)
    sinks = [jnp.full((1, 1), sink_ref[kvh * WIN_GROUP + j], F32) for j in range(WIN_GROUP)]

    def attend(qs, sks, locs):
        rowmax = lambda s: jnp.max(s, axis=-1, keepdims=True)
        rowsum = lambda e: jnp.sum(e, axis=-1, keepdims=True)
        s_ctxs = [_bdot_nt(q, kc_s[...]) * scale for q in qs]
        ms = _each(lambda s, sk: jnp.maximum(rowmax(s), sk), s_ctxs, sks)
        if locs is not None:
            s_locs = _each(lambda q, loc: jnp.where(loc[2], _bdot_nt(q, loc[0]) * scale, NEG_BIG), qs, locs)
            ms = _each(lambda m, s: jnp.maximum(m, rowmax(s)), ms, s_locs)
        e_ctxs = _each(lambda s, m: jnp.exp(s - m), s_ctxs, ms)
        dens = _each(lambda e, sk, m: rowsum(e) + jnp.exp(sk - m), e_ctxs, sks, ms)
        os_ = [_bdot(e, vc_s[...]) for e in e_ctxs]
        if locs is not None:
            e_locs = _each(lambda s, m: jnp.exp(s - m), s_locs, ms)
            dens = _each(lambda den, e: den + rowsum(e), dens, e_locs)
            os_ = _each(lambda o, e, loc: o + _bdot(e, loc[1]), os_, e_locs, locs)
        return _each(lambda o, den: (o / den).astype(BF16), os_, dens)

    def q_blocks(it, carry):
        qs, sks, locs, dst = [], [], [], []
        for b in range(WIN_BLOCK_GROUP):
            n = it * WIN_BLOCK_GROUP + b
            q0 = pl.multiple_of(n * wb, wb)
            k0 = pl.multiple_of(jnp.clip(n * wb - wb, 0, seq - band), wb)
            qpos = q0 + lax.broadcasted_iota(jnp.int32, (wb, band), 0)
            kpos = k0 + lax.broadcasted_iota(jnp.int32, (wb, band), 1)
            loc = (k_s[pl.ds(k0, band), :], v_s[pl.ds(k0, band), :], jnp.abs(qpos - kpos) <= WIN)
            cos = cos_ref[pl.ds(q0, wb), :]
            sin = sin_ref[pl.ds(q0, wb), :]
            for j in range(WIN_GROUP):
                qs.append(_rope(q_ref[pl.ds(q0, wb), j * HEAD_DIM:(j + 1) * HEAD_DIM], cos, sin, HEAD_DIM // 4))
                sks.append(sinks[j])
                locs.append(loc)
                dst.append((q0, j))
        for (q0, j), o in zip(dst, attend(qs, sks, locs)):
            o_ref[pl.ds(q0, wb), j * HEAD_DIM:(j + 1) * HEAD_DIM] = o
        return carry

    lax.fori_loop(0, seq // (wb * WIN_BLOCK_GROUP), q_blocks, 0)
    if need_ctx:
        qs = [qc_ref[:, j * HEAD_DIM:(j + 1) * HEAD_DIM] for j in range(WIN_GROUP)]
        for j, o in enumerate(attend(qs, sinks, None)):
            oc_ref[:, j * HEAD_DIM:(j + 1) * HEAD_DIM] = o


def win_mixer(p, cos, sin, sink, n_batch, seq, n_ctx, need_ctx):
    nh = N_HEADS_HALF
    cb = (n_batch * seq) // n_ctx
    gw = WIN_GROUP * HEAD_DIM
    q_blk0 = (3 * nh * HEAD_DIM) // gw
    in_specs = [
        pl.BlockSpec((seq, gw), lambda b, g: (b, q_blk0 + g)),
        pl.BlockSpec((seq, HEAD_DIM), lambda b, g: (b, 4 * nh + g)),
        pl.BlockSpec((seq, HEAD_DIM), lambda b, g: (b, 4 * nh + WIN_KV_HEADS + g)),
        pl.BlockSpec((n_ctx, HEAD_DIM), lambda b, g: (cb + b, 4 * nh + g)),
        pl.BlockSpec((n_ctx, HEAD_DIM), lambda b, g: (cb + b, 4 * nh + WIN_KV_HEADS + g)),
        pl.BlockSpec((seq, HEAD_DIM), lambda b, g: (0, 0)),
        pl.BlockSpec((seq, HEAD_DIM), lambda b, g: (0, 0)),
        pl.BlockSpec(memory_space=pltpu.SMEM),
    ]
    args = [p, p, p, p, p, cos, sin, sink]
    out_specs = [pl.BlockSpec((seq, gw), lambda b, g: (b, g))]
    out_shape = [jax.ShapeDtypeStruct((n_batch * seq, HALF_W), BF16)]
    if need_ctx:
        in_specs.append(pl.BlockSpec((n_ctx, gw), lambda b, g: (cb + b, q_blk0 + g)))
        args.append(p)
        out_specs.append(pl.BlockSpec((n_ctx, gw), lambda b, g: (b, g)))
        out_shape.append(jax.ShapeDtypeStruct((n_batch * n_ctx, HALF_W), BF16))
    outs = pl.pallas_call(
        functools.partial(_win_kernel, seq=seq, n_ctx=n_ctx, need_ctx=need_ctx),
        grid=(n_batch, WIN_KV_HEADS),
        in_specs=in_specs,
        out_specs=out_specs,
        out_shape=out_shape,
        scratch_shapes=[pltpu.VMEM((seq, HEAD_DIM), BF16), pltpu.VMEM((seq, HEAD_DIM), BF16),
                        pltpu.VMEM((n_ctx, HEAD_DIM), BF16), pltpu.VMEM((n_ctx, HEAD_DIM), BF16)],
        compiler_params=_cparams(("arbitrary", "arbitrary")),
        name="win_attn",
    )(*args)
    return (outs[0], outs[1]) if need_ctx else (outs[0], None)


def kernel(x, c, ctx, c_ctx, ada_w, ada_b, norm_mix_g, norm_ffn_g, w_in_even, gdn_conv_w, gdn_a_log, gdn_dt_bias,
           gdn_norm_g, diff_lambda_q1, diff_lambda_k1, diff_lambda_q2, diff_lambda_k2, diff_subln_g, w_in_odd,
           na_rpb, win_sink, w_out, ffn_w_gate, ffn_w_up, ffn_w_down, final_norm_g):
    n_batch, seq, d = x.shape
    n_ctx = ctx.shape[1]
    n_lat = n_batch * seq
    assert n_batch + 1 <= MOD_ROWS and seq % GRID_W == 0
    hf = jnp.concatenate([x.reshape(n_lat, d), ctx.reshape(n_batch * n_ctx, d)], axis=0)
    cc = jnp.zeros((MOD_ROWS, d), F32).at[:n_batch].set(c).at[n_batch].set(c_ctx)
    mods = ada_mod(cc, ada_w, ada_b)

    cos64, sin64 = _rope_tables(seq, DIFF_DH)
    cos128, sin128 = _rope_tables(seq, HEAD_DIM)

    qkvz = 4 * HALF_W
    for l in range(DEPTH):
        need_ctx = l < DEPTH - 1
        i = l // 2
        if l % 2 == 0:
            w = w_in_even[i]
            w_main = jnp.concatenate([w[:, :qkvz], w[:, qkvz + 4 * N_HEADS_HALF:]], axis=1).astype(BF16)
            w_ab = jnp.pad(w[:, qkvz:qkvz + 4 * N_HEADS_HALF], ((0, 0), (0, HEAD_DIM - 4 * N_HEADS_HALF))).astype(BF16)
            p, ab = in_proj(hf, mods[l], norm_mix_g[l], w_main, w_ab, n_lat, seq, 1024)
            lambda_init = 0.8 - 0.6 * float(np.exp(-0.3 * l))
            ya = gdn_mixer(p, ab, gdn_conv_w[i], gdn_a_log[i], gdn_dt_bias[i], gdn_norm_g[i],
                           n_batch, seq, n_ctx, need_ctx)
            yb = diff_mixer(p, cos64, sin64, diff_lambda_q1[i], diff_lambda_k1[i], diff_lambda_q2[i],
                            diff_lambda_k2[i], diff_subln_g[i], lambda_init, n_batch, seq, n_ctx, need_ctx)
        else:
            p = in_proj(hf, mods[l], norm_mix_g[l], w_in_odd[i].astype(BF16), None, n_lat, seq, 1536)
            bias_tab = _na_bias_table(na_rpb[i], seq // GRID_W)
            ya = na_mixer(p, bias_tab, n_batch, seq, n_ctx, need_ctx)
            yb = win_mixer(p, cos128, sin128, win_sink[i], n_batch, seq, n_ctx, need_ctx)
        y_ctx = (ya[1], yb[1]) if need_ctx else None
        h1, u2 = out_proj((ya[0], yb[0]), y_ctx, hf, mods[l], norm_ffn_g[l], w_out[l].astype(BF16), n_lat, seq)
        hf = ffn_split(u2, h1, mods[l], ffn_w_gate[l].astype(BF16), ffn_w_up[l].astype(BF16),
                       ffn_w_down[l].astype(BF16), n_lat, seq)
    return final_norm(hf, final_norm_g).reshape(n_batch, seq, d)
```
